```python
import jax, jax.numpy as jnp
from jax import lax
import numpy as np

D_MODEL = 1024
BATCH = 4
SEQ = 8192
DEPTH = 4

D_RNN = D_MODEL
RG_HEADS = 4
RG_BLOCK = D_RNN // RG_HEADS
RG_C = 8.0
CONV4_WIDTH = 4
D_CONV = D_MODEL
CONV3_WIDTH = 3
D_FF = ((8 * D_MODEL + 3 * 256 - 1) // (3 * 256)) * 256
PLE_DIM = 256
EPS = 1e-6
IN_SPLITS = (D_RNN, D_RNN, D_CONV, D_CONV, D_CONV, D_MODEL, D_MODEL)
W_IN = sum(IN_SPLITS)
IN_OFFSETS = tuple(int(v) for v in np.cumsum(IN_SPLITS)[:-1])

kernel_name = "hybrid_rglru_shortconv_block"


def rmsnorm(x, g):
    xf = x.astype(jnp.float32)
    y = xf * lax.rsqrt(jnp.mean(xf * xf, axis=-1, keepdims=True) + EPS)
    return (y * g.astype(jnp.float32)).astype(x.dtype)


def causal_depthwise_conv(x, w):
    k_w = w.shape[0]
    t = x.shape[1]
    xp = jnp.pad(x, ((0, 0), (k_w - 1, 0), (0, 0)))
    y = xp[:, 0:t] * w[0]
    for k in range(1, k_w):
        y = y + xp[:, k:k + t] * w[k]
    return y


def block_diag_linear(x, w, b):
    bsz, t, _ = x.shape
    xh = x.reshape(bsz, t, RG_HEADS, RG_BLOCK)
    y = jnp.einsum("bthi,hij->bthj", xh, w) + b
    return y.reshape(bsz, t, RG_HEADS * RG_BLOCK)


def rg_lru(x, w_r, b_r, w_i, b_i, lam):
    r = jax.nn.sigmoid(block_diag_linear(x, w_r, b_r).astype(jnp.float32))
    i = jax.nn.sigmoid(block_diag_linear(x, w_i, b_i).astype(jnp.float32))
    log_a = -RG_C * r * jax.nn.softplus(-lam.astype(jnp.float32))
    a = jnp.exp(log_a)
    mult = jnp.sqrt(-jnp.expm1(2.0 * log_a))
    u = mult * (i * x.astype(jnp.float32))

    def step(h, au):
        a_t, u_t = au
        h = a_t * h + u_t
        return h, h

    h0 = jnp.zeros((x.shape[0], x.shape[2]), jnp.float32)
    _, hs = lax.scan(step, h0, (jnp.swapaxes(a, 0, 1), jnp.swapaxes(u, 0, 1)))
    return jnp.swapaxes(hs, 0, 1).astype(x.dtype)


def hybrid_layer(x, p_i, g_mix, w_in, conv4_w, conv4_b, w_rg_r, b_rg_r, w_rg_i, b_rg_i,
                 lru_lambda, conv3_w, w_out, g_ffn, w_gate_up, w_down, g_ple, w_ple_gate, w_ple):
    h = rmsnorm(x, g_mix)
    z = h @ w_in
    rnn_x, rnn_y, conv_b, conv_c, conv_x, gate_rnn, gate_conv = jnp.split(z, IN_OFFSETS, axis=-1)
    rnn_x = causal_depthwise_conv(rnn_x, conv4_w) + conv4_b
    y_rnn = jax.nn.gelu(rnn_y) * rg_lru(rnn_x, w_rg_r, b_rg_r, w_rg_i, b_rg_i, lru_lambda)
    y_conv = conv_b * causal_depthwise_conv(conv_c * conv_x, conv3_w)
    merged = jax.nn.sigmoid(gate_rnn) * y_rnn + jax.nn.sigmoid(gate_conv) * y_conv
    x = x + merged @ w_out
    h = rmsnorm(x, g_ffn)
    g, u = jnp.split(h @ w_gate_up, 2, axis=-1)
    x = x + (jax.nn.silu(g) * u) @ w_down
    gate = jax.nn.sigmoid(rmsnorm(x, g_ple) @ w_ple_gate)
    x = x + gate * (p_i @ w_ple)
    return x


def setup_inputs(seed: int = 0) -> dict:
    key = jax.random.key(seed)
    ks = jax.random.split(key, 20)

    def nrm(k, shape, fan_in):
        return jax.random.normal(k, shape, jnp.float32) * (fan_in ** -0.5)

    def gain(k, shape):
        return 1.0 + 0.05 * jax.random.normal(k, shape, jnp.float32)

    def bias(k, shape):
        return 0.02 * jax.random.normal(k, shape, jnp.float32)

    x = jax.random.normal(ks[0], (BATCH, SEQ, D_MODEL), jnp.float32)
    p = jax.random.normal(ks[1], (DEPTH, BATCH, SEQ, PLE_DIM), jnp.float32)
    g_mix = gain(ks[2], (DEPTH, D_MODEL))
    w_in = nrm(ks[3], (DEPTH, D_MODEL, W_IN), D_MODEL)
    conv4_w = nrm(ks[4], (DEPTH, CONV4_WIDTH, D_RNN), CONV4_WIDTH)
    conv4_b = bias(ks[5], (DEPTH, D_RNN))
    w_rg_r = nrm(ks[6], (DEPTH, RG_HEADS, RG_BLOCK, RG_BLOCK), RG_BLOCK)
    b_rg_r = bias(ks[7], (DEPTH, RG_HEADS, RG_BLOCK))
    w_rg_i = nrm(ks[8], (DEPTH, RG_HEADS, RG_BLOCK, RG_BLOCK), RG_BLOCK)
    b_rg_i = bias(ks[9], (DEPTH, RG_HEADS, RG_BLOCK))
    a_c = jax.random.uniform(ks[10], (DEPTH, D_RNN), jnp.float32, minval=0.9, maxval=0.999)
    a0 = a_c ** (1.0 / RG_C)
    lru_lambda = jnp.log(a0) - jnp.log1p(-a0)
    conv3_w = nrm(ks[11], (DEPTH, CONV3_WIDTH, D_CONV), CONV3_WIDTH)
    w_out = nrm(ks[12], (DEPTH, D_MODEL, D_MODEL), D_MODEL)
    g_ffn = gain(ks[13], (DEPTH, D_MODEL))
    w_gate_up = nrm(ks[14], (DEPTH, D_MODEL, 2 * D_FF), D_MODEL)
    w_down = nrm(ks[15], (DEPTH, D_FF, D_MODEL), D_FF)
    g_ple = gain(ks[16], (DEPTH, D_MODEL))
    w_ple_gate = nrm(ks[17], (DEPTH, D_MODEL, D_MODEL), D_MODEL)
    w_ple = nrm(ks[18], (DEPTH, PLE_DIM, D_MODEL), PLE_DIM)
    g_final = gain(ks[19], (D_MODEL,))
    return {"x": x, "p": p, "g_mix": g_mix, "w_in": w_in, "conv4_w": conv4_w,
            "conv4_b": conv4_b, "w_rg_r": w_rg_r, "b_rg_r": b_rg_r, "w_rg_i": w_rg_i,
            "b_rg_i": b_rg_i, "lru_lambda": lru_lambda, "conv3_w": conv3_w, "w_out": w_out,
            "g_ffn": g_ffn, "w_gate_up": w_gate_up, "w_down": w_down, "g_ple": g_ple,
            "w_ple_gate": w_ple_gate, "w_ple": w_ple, "g_final": g_final}


def reference(x, p, g_mix, w_in, conv4_w, conv4_b, w_rg_r, b_rg_r, w_rg_i, b_rg_i,
              lru_lambda, conv3_w, w_out, g_ffn, w_gate_up, w_down, g_ple, w_ple_gate,
              w_ple, g_final):
    for i in range(DEPTH):
        x = hybrid_layer(x, p[i], g_mix[i], w_in[i], conv4_w[i], conv4_b[i], w_rg_r[i],
                         b_rg_r[i], w_rg_i[i], b_rg_i[i], lru_lambda[i], conv3_w[i], w_out[i],
                         g_ffn[i], w_gate_up[i], w_down[i], g_ple[i], w_ple_gate[i], w_ple[i])
    return rmsnorm(x, g_final)
```

```python
import functools

import jax
import jax.numpy as jnp
from jax import lax
from jax.experimental import pallas as pl
from jax.experimental.pallas import tpu as pltpu

EPS = 1e-6
RG_C = 8.0
RG_HEADS = 4
CONV4_WIDTH = 4
CONV3_WIDTH = 3

SUBLANES = 8
SUB = 128
J = SUB // SUBLANES
TILE = 512
VMEM_LIMIT_BYTES = 60000 * 1024


def _permute_rows(a):
    n, c = a.shape
    return a.reshape(n // SUB, SUBLANES, J, c).transpose(0, 2, 1, 3).reshape(n, c)


def _unpermute_rows(a):
    n, c = a.shape
    return a.reshape(n // SUB, J, SUBLANES, c).transpose(0, 2, 1, 3).reshape(n, c)


def _rmsnorm(xs, g):
    ms = jnp.mean(xs * xs, axis=-1, keepdims=True)
    return (xs * lax.rsqrt(ms + EPS)) * g


def _softplus(z):
    return jnp.maximum(z, 0.0) + jnp.log1p(jnp.exp(-jnp.abs(z)))


def _rows(j, n=1):
    return slice(j * SUBLANES, (j + n) * SUBLANES)


def _mixer_kernel(x_ref, gmix_ref, win_ref, c4w_ref, c4b_ref, wr_ref, br_ref, wi_ref,
                  bi_ref, lam_ref, c3w_ref, wout_ref, o_ref,
                  z_ref, xc_ref, r_ref, i_ref, hl_ref, pl_ref, q_ref, mrg_ref,
                  p4_ref, pq_ref, hc_ref, c_ref, *, tiles_per_seq, d):
    g = pl.program_id(0)

    @pl.when(g % tiles_per_seq == 0)
    def _():
        p4_ref[...] = jnp.zeros_like(p4_ref)
        pq_ref[...] = jnp.zeros_like(pq_ref)
        hc_ref[...] = jnp.zeros_like(hc_ref)

    hb = d // RG_HEADS
    o_rx, o_ry, o_cb, o_cc, o_cx, o_gr, o_gc = (k * d for k in range(7))
    first_stream = lax.broadcasted_iota(jnp.int32, (SUBLANES, d), 0) == 0

    def from_prev_stream(cur, prev):
        return jnp.where(first_stream, pltpu.roll(prev, 1, 0), pltpu.roll(cur, 1, 0))

    def sub_block(m, carry):
        rows = pl.ds(pl.multiple_of(m * SUB, SUB), SUB)
        xs = x_ref[rows, :]
        hn = _rmsnorm(xs, gmix_ref[...]).astype(jnp.bfloat16)
        z_ref[...] = jnp.dot(hn, win_ref[...], preferred_element_type=jnp.float32)

        halo4 = [from_prev_stream(z_ref[_rows(J - 3 + jj), o_rx:o_rx + d], p4_ref[_rows(jj), :])
                 for jj in range(CONV4_WIDTH - 1)]
        p4_ref[...] = z_ref[_rows(J - 3, 3), o_rx:o_rx + d]

        def rx(j):
            return halo4[j + 3] if j < 0 else z_ref[_rows(j), o_rx:o_rx + d]

        for j in range(J):
            acc = c4b_ref[...] + c4w_ref[0:1, :] * rx(j - 3)
            for k in range(1, CONV4_WIDTH):
                acc = acc + c4w_ref[k:k + 1, :] * rx(j - 3 + k)
            xc_ref[_rows(j), :] = acc

        for h in range(RG_HEADS):
            cols = slice(h * hb, (h + 1) * hb)
            xh = xc_ref[:, cols].astype(jnp.bfloat16)
            r_ref[:, cols] = jnp.dot(xh, wr_ref[h], preferred_element_type=jnp.float32)
            i_ref[:, cols] = jnp.dot(xh, wi_ref[h], preferred_element_type=jnp.float32)

        log_a_scale = -RG_C * _softplus(-lam_ref[...])
        hloc = jnp.zeros((SUBLANES, d), jnp.float32)
        prod = jnp.ones((SUBLANES, d), jnp.float32)
        for j in range(J):
            r = jax.nn.sigmoid(r_ref[_rows(j), :] + br_ref[...])
            gi = jax.nn.sigmoid(i_ref[_rows(j), :] + bi_ref[...])
            log_a = log_a_scale * r
            a = jnp.exp(log_a)
            mult = jnp.sqrt(-jnp.tanh(log_a) * (1.0 + a * a))
            u = mult * (gi * xc_ref[_rows(j), :])
            hloc = a * hloc + u
            prod = a * prod
            hl_ref[_rows(j), :] = hloc
            pl_ref[_rows(j), :] = prod

        cur = hc_ref[0:1, :]
        last = (J - 1) * SUBLANES
        for s in range(SUBLANES):
            c_ref[s:s + 1, :] = cur
            cur = hl_ref[last + s:last + s + 1, :] + pl_ref[last + s:last + s + 1, :] * cur
        hc_ref[0:1, :] = cur
        cvec = c_ref[...]

        for j in range(J):
            q_ref[_rows(j), :] = z_ref[_rows(j), o_cc:o_cc + d] * z_ref[_rows(j), o_cx:o_cx + d]
        halo3 = [from_prev_stream(q_ref[_rows(J - 2 + jj), :], pq_ref[_rows(jj), :])
                 for jj in range(CONV3_WIDTH - 1)]
        pq_ref[...] = q_ref[_rows(J - 2, 2), :]

        def qv(j):
            return halo3[j + 2] if j < 0 else q_ref[_rows(j), :]

        for j in range(J):
            hfull = hl_ref[_rows(j), :] + pl_ref[_rows(j), :] * cvec
            y_rnn = jax.nn.gelu(z_ref[_rows(j), o_ry:o_ry + d]) * hfull
            yc = c3w_ref[0:1, :] * qv(j - 2)
            for k in range(1, CONV3_WIDTH):
                yc = yc + c3w_ref[k:k + 1, :] * qv(j - 2 + k)
            y_conv = z_ref[_rows(j), o_cb:o_cb + d] * yc
            mrg_ref[_rows(j), :] = (
                jax.nn.sigmoid(z_ref[_rows(j), o_gr:o_gr + d]) * y_rnn
                + jax.nn.sigmoid(z_ref[_rows(j), o_gc:o_gc + d]) * y_conv)

        o_ref[rows, :] = xs + jnp.dot(mrg_ref[...].astype(jnp.bfloat16), wout_ref[...],
                                      preferred_element_type=jnp.float32)
        return carry

    lax.fori_loop(0, TILE // SUB, sub_block, 0)


def _ffn_kernel(x_ref, p_ref, gffn_ref, wgu_ref, wd_ref, gple_ref, wpg_ref, wple_ref,
                gfin_ref, o_ref, gu_ref, *, d_ff, final):
    def sub_block(m, carry):
        rows = pl.ds(pl.multiple_of(m * SUB, SUB), SUB)
        xs = x_ref[rows, :]
        hn = _rmsnorm(xs, gffn_ref[...]).astype(jnp.bfloat16)
        gu_ref[...] = jnp.dot(hn, wgu_ref[...], preferred_element_type=jnp.float32)
        act = (jax.nn.silu(gu_ref[:, 0:d_ff]) * gu_ref[:, d_ff:2 * d_ff]).astype(jnp.bfloat16)
        x2 = xs + jnp.dot(act, wd_ref[...], preferred_element_type=jnp.float32)

        hn2 = _rmsnorm(x2, gple_ref[...]).astype(jnp.bfloat16)
        gate = jax.nn.sigmoid(jnp.dot(hn2, wpg_ref[...], preferred_element_type=jnp.float32))
        pe = jnp.dot(p_ref[rows, :].astype(jnp.bfloat16), wple_ref[...],
                     preferred_element_type=jnp.float32)
        x3 = x2 + gate * pe
        if final:
            x3 = _rmsnorm(x3, gfin_ref[...])
        o_ref[rows, :] = x3
        return carry

    lax.fori_loop(0, TILE // SUB, sub_block, 0)


def _resident(shape):
    zeros = (0,) * len(shape)
    return pl.BlockSpec(shape, lambda g: zeros, pipeline_mode=pl.Buffered(1))


def _mixer_call(x2d, gmix, win, c4w, c4b, wr, br, wi, bi, lam, c3w, wout, *, seq_len):
    n, d = x2d.shape
    w_in_cols = win.shape[1]
    f32 = jnp.float32
    row_spec = pl.BlockSpec((TILE, d), lambda g: (g, 0))
    consts = (gmix, win, c4w, c4b, wr, br, wi, bi, lam, c3w, wout)
    return pl.pallas_call(
        functools.partial(_mixer_kernel, tiles_per_seq=seq_len // TILE, d=d),
        grid=(n // TILE,),
        in_specs=[row_spec] + [_resident(c.shape) for c in consts],
        out_specs=row_spec,
        out_shape=jax.ShapeDtypeStruct((n, d), f32),
        scratch_shapes=[
            pltpu.VMEM((SUB, w_in_cols), f32),
            pltpu.VMEM((SUB, d), f32),
            pltpu.VMEM((SUB, d), f32),
            pltpu.VMEM((SUB, d), f32),
            pltpu.VMEM((SUB, d), f32),
            pltpu.VMEM((SUB, d), f32),
            pltpu.VMEM((SUB, d), f32),
            pltpu.VMEM((SUB, d), f32),
            pltpu.VMEM(((CONV4_WIDTH - 1) * SUBLANES, d), f32),
            pltpu.VMEM(((CONV3_WIDTH - 1) * SUBLANES, d), f32),
            pltpu.VMEM((SUBLANES, d), f32),
            pltpu.VMEM((SUBLANES, d), f32),
        ],
        compiler_params=pltpu.CompilerParams(
            dimension_semantics=("arbitrary",), vmem_limit_bytes=VMEM_LIMIT_BYTES),
        name="mixer",
    )(x2d, *consts)


def _ffn_call(x2d, p2d, gffn, wgu, wd, gple, wpg, wple, gfin, *, final):
    n, d = x2d.shape
    d_ff = wd.shape[0]
    f32 = jnp.float32
    row_spec = pl.BlockSpec((TILE, d), lambda g: (g, 0))
    p_spec = pl.BlockSpec((TILE, p2d.shape[1]), lambda g: (g, 0))
    consts = (gffn, wgu, wd, gple, wpg, wple, gfin)
    return pl.pallas_call(
        functools.partial(_ffn_kernel, d_ff=d_ff, final=final),
        grid=(n // TILE,),
        in_specs=[row_spec, p_spec] + [_resident(c.shape) for c in consts],
        out_specs=row_spec,
        out_shape=jax.ShapeDtypeStruct((n, d), f32),
        scratch_shapes=[pltpu.VMEM((SUB, 2 * d_ff), f32)],
        compiler_params=pltpu.CompilerParams(
            dimension_semantics=("arbitrary",), vmem_limit_bytes=VMEM_LIMIT_BYTES),
        name="ffn_ple",
    )(x2d, p2d, *consts)


def kernel(x, p, g_mix, w_in, conv4_w, conv4_b, w_rg_r, b_rg_r, w_rg_i, b_rg_i, lru_lambda, conv3_w, w_out, g_ffn, w_gate_up, w_down, g_ple, w_ple_gate, w_ple, g_final):
    bsz, seq_len, d = x.shape
    depth = p.shape[0]
    assert seq_len % TILE == 0 and TILE % SUB == 0 and d % (RG_HEADS * 128) == 0
    bf16 = jnp.bfloat16
    row = lambda v: v.reshape(1, -1)

    xs = _permute_rows(x.reshape(bsz * seq_len, d))
    for i in range(depth):
        xs = _mixer_call(
            xs, row(g_mix[i]), w_in[i].astype(bf16), conv4_w[i], row(conv4_b[i]),
            w_rg_r[i].astype(bf16), row(b_rg_r[i]), w_rg_i[i].astype(bf16), row(b_rg_i[i]),
            row(lru_lambda[i]), conv3_w[i], w_out[i].astype(bf16), seq_len=seq_len)
        p_i = _permute_rows(p[i].reshape(bsz * seq_len, -1))
        xs = _ffn_call(
            xs, p_i, row(g_ffn[i]), w_gate_up[i].astype(bf16), w_down[i].astype(bf16),
            row(g_ple[i]), w_ple_gate[i].astype(bf16), w_ple[i].astype(bf16),
            row(g_final), final=(i == depth - 1))
    return _unpermute_rows(xs).reshape(bsz, seq_len, d)
```

```python
import functools

import jax
import jax.numpy as jnp
from jax import lax
from jax.experimental import pallas as pl
from jax.experimental.pallas import tpu as pltpu

EPS = 1e-6
RG_C = 8.0
RG_HEADS = 4
CONV4_WIDTH = 4
CONV3_WIDTH = 3

SUBLANES = 8
SUB = 128
J = SUB // SUBLANES
TILE = 2 * SUB
VMEM_LIMIT_BYTES = 60000 * 1024

_H3 = CONV3_WIDTH - 1
_H4 = CONV4_WIDTH - 1


def _permute_rows(a):
    n, c = a.shape
    return a.reshape(n // SUB, SUBLANES, J, c).transpose(0, 2, 1, 3).reshape(n, c)


def _unpermute_rows(a):
    n, c = a.shape
    return a.reshape(n // SUB, J, SUBLANES, c).transpose(0, 2, 1, 3).reshape(n, c)


def _rmsnorm(xs, g):
    ms = jnp.mean(xs * xs, axis=-1, keepdims=True)
    return (xs * lax.rsqrt(ms + EPS)) * g


def _softplus(z):
    return jnp.maximum(z, 0.0) + jnp.log1p(jnp.exp(-jnp.abs(z)))


def _rows(j, n=1):
    return slice(j * SUBLANES, (j + n) * SUBLANES)


def _pack_weights(w):
    wb = w.astype(jnp.bfloat16)
    *lead, k, n = wb.shape
    pairs = jnp.swapaxes(wb.reshape(*lead, k // 2, 2, n), -1, -2)
    return lax.bitcast_convert_type(pairs, jnp.uint32)


def _weights(packed):
    return pltpu.bitcast(packed, jnp.bfloat16)


def _mixer_kernel(x_ref, xn_ref, gmix_ref, win_ref, c4w_ref, c4b_ref, wr_ref, br_ref,
                  wi_ref, bi_ref, lam_ref, c3w_ref, wout_ref, o_ref,
                  z0_ref, z1_ref, xc_ref, q0_ref, q1_ref,
                  a_ref, u_ref, hl_ref, pl_ref, mrg_ref,
                  p4_ref, pq_ref, hc_ref, c_ref, *, groups_per_seq, d):
    g = pl.program_id(0)
    hb = d // RG_HEADS
    o_rx, o_ry, o_cb, o_cc, o_cx, o_gr, o_gc = (k * d for k in range(7))
    first_stream = lax.broadcasted_iota(jnp.int32, (SUBLANES, d), 0) == 0

    def from_prev_stream(cur, prev):
        return jnp.where(first_stream, pltpu.roll(prev, 1, 0), pltpu.roll(cur, 1, 0))

    def project(xs, z_ref):
        hn = _rmsnorm(xs, gmix_ref[...]).astype(jnp.bfloat16)
        z_ref[...] = jnp.dot(hn, _weights(win_ref[...]), preferred_element_type=jnp.float32)

    def convs(z_ref, q_ref, seq_start):
        def history(h_ref, jj):
            return jnp.where(seq_start, 0.0, h_ref[_rows(jj), :])

        halo4 = [from_prev_stream(z_ref[_rows(J - _H4 + jj), o_rx:o_rx + d],
                                  history(p4_ref, jj)) for jj in range(_H4)]
        p4_ref[...] = z_ref[_rows(J - _H4, _H4), o_rx:o_rx + d]

        def rx(j):
            return halo4[j + _H4] if j < 0 else z_ref[_rows(j), o_rx:o_rx + d]

        for j in range(J):
            acc = c4b_ref[...] + c4w_ref[_rows(0), :] * rx(j - _H4)
            for k in range(1, CONV4_WIDTH):
                acc = acc + c4w_ref[_rows(k), :] * rx(j - _H4 + k)
            xc_ref[_rows(j), :] = acc

        for j in range(J):
            q_ref[_rows(_H3 + j), :] = (z_ref[_rows(j), o_cc:o_cc + d]
                                        * z_ref[_rows(j), o_cx:o_cx + d])
        for jj in range(_H3):
            q_ref[_rows(jj), :] = from_prev_stream(q_ref[_rows(J + jj), :],
                                                   history(pq_ref, jj))
        pq_ref[...] = q_ref[_rows(J, _H3), :]

    def gates():
        for h in range(RG_HEADS):
            cols = slice(h * hb, (h + 1) * hb)
            xh = xc_ref[:, cols]
            xb = xh.astype(jnp.bfloat16)
            r_pre = jnp.dot(xb, _weights(wr_ref[h]), preferred_element_type=jnp.float32)
            i_pre = jnp.dot(xb, _weights(wi_ref[h]), preferred_element_type=jnp.float32)
            log_a_scale = -RG_C * _softplus(-lam_ref[:, cols])
            for j in range(J):
                r = jax.nn.sigmoid(r_pre[_rows(j), :] + br_ref[:, cols])
                gi = jax.nn.sigmoid(i_pre[_rows(j), :] + bi_ref[:, cols])
                log_a = log_a_scale * r
                a = jnp.exp(log_a)
                mult = jnp.sqrt(-jnp.tanh(log_a) * (1.0 + a * a))
                a_ref[_rows(j), cols] = a
                u_ref[_rows(j), cols] = mult * (gi * xh[_rows(j), :])

    def scan_merge(z_ref, q_ref, seq_start):
        hloc = jnp.zeros((SUBLANES, d), jnp.float32)
        prod = jnp.ones((SUBLANES, d), jnp.float32)
        for j in range(J):
            a = a_ref[_rows(j), :]
            hloc = a * hloc + u_ref[_rows(j), :]
            prod = a * prod
            hl_ref[_rows(j), :] = hloc
            pl_ref[_rows(j), :] = prod

        cur = jnp.where(seq_start, 0.0, hc_ref[0:1, :])
        last = (J - 1) * SUBLANES
        for s in range(SUBLANES):
            c_ref[s:s + 1, :] = cur
            cur = hl_ref[last + s:last + s + 1, :] + pl_ref[last + s:last + s + 1, :] * cur
        hc_ref[0:1, :] = cur
        cvec = c_ref[...]

        for j in range(J):
            hfull = hl_ref[_rows(j), :] + pl_ref[_rows(j), :] * cvec
            y_rnn = jax.nn.gelu(z_ref[_rows(j), o_ry:o_ry + d]) * hfull
            yc = c3w_ref[_rows(0), :] * q_ref[_rows(j), :]
            for k in range(1, CONV3_WIDTH):
                yc = yc + c3w_ref[_rows(k), :] * q_ref[_rows(j + k), :]
            y_conv = z_ref[_rows(j), o_cb:o_cb + d] * yc
            mrg_ref[_rows(j), :] = (
                jax.nn.sigmoid(z_ref[_rows(j), o_gr:o_gr + d]) * y_rnn
                + jax.nn.sigmoid(z_ref[_rows(j), o_gc:o_gc + d]) * y_conv)

    def output(xs):
        return xs + jnp.dot(mrg_ref[...].astype(jnp.bfloat16), _weights(wout_ref[...]),
                            preferred_element_type=jnp.float32)

    group0 = 2 * g
    starts = [(group0 + k) % groups_per_seq == 0 for k in range(3)]

    @pl.when(g == 0)
    def _():
        p4_ref[...] = jnp.zeros_like(p4_ref)
        pq_ref[...] = jnp.zeros_like(pq_ref)
        hc_ref[...] = jnp.zeros_like(hc_ref)
        project(x_ref[0:SUB, :], z0_ref)
        convs(z0_ref, q0_ref, True)

    gates()
    project(x_ref[SUB:TILE, :], z1_ref)
    scan_merge(z0_ref, q0_ref, starts[0])
    o_ref[0:SUB, :] = output(x_ref[0:SUB, :])
    convs(z1_ref, q1_ref, starts[1])

    gates()
    project(xn_ref[...], z0_ref)
    scan_merge(z1_ref, q1_ref, starts[1])
    o_ref[SUB:TILE, :] = output(x_ref[SUB:TILE, :])
    convs(z0_ref, q0_ref, starts[2])


def _ffn_kernel(x_ref, xn_ref, p_ref, gffn_ref, wgu_ref, wd_ref, gple_ref, wpg_ref,
                wple_ref, gfin_ref, o_ref, gu_ref, act0_ref, act1_ref, *, d_ff, final):
    g = pl.program_id(0)

    def project(xs):
        hn = _rmsnorm(xs, gffn_ref[...]).astype(jnp.bfloat16)
        gu_ref[...] = jnp.dot(hn, _weights(wgu_ref[...]), preferred_element_type=jnp.float32)

    def activate(act_ref):
        act_ref[...] = (jax.nn.silu(gu_ref[:, 0:d_ff])
                        * gu_ref[:, d_ff:2 * d_ff]).astype(jnp.bfloat16)

    def down(act_ref, xs):
        return xs + jnp.dot(act_ref[...], _weights(wd_ref[...]), preferred_element_type=jnp.float32)

    def embed(x2, ps):
        hn2 = _rmsnorm(x2, gple_ref[...]).astype(jnp.bfloat16)
        gate = jax.nn.sigmoid(jnp.dot(hn2, _weights(wpg_ref[...]), preferred_element_type=jnp.float32))
        pe = jnp.dot(ps.astype(jnp.bfloat16), _weights(wple_ref[...]), preferred_element_type=jnp.float32)
        x3 = x2 + gate * pe
        if final:
            x3 = _rmsnorm(x3, gfin_ref[...])
        return x3

    @pl.when(g == 0)
    def _():
        project(x_ref[0:SUB, :])
        activate(act0_ref)

    x2 = down(act0_ref, x_ref[0:SUB, :])
    project(x_ref[SUB:TILE, :])
    activate(act1_ref)
    o_ref[0:SUB, :] = embed(x2, p_ref[0:SUB, :])

    x2 = down(act1_ref, x_ref[SUB:TILE, :])
    project(xn_ref[...])
    activate(act0_ref)
    o_ref[SUB:TILE, :] = embed(x2, p_ref[SUB:TILE, :])


def _resident(shape):
    zeros = (0,) * len(shape)
    return pl.BlockSpec(shape, lambda g: zeros, pipeline_mode=pl.Buffered(1))


def _row_specs(n, d):
    last_group = n // SUB - 1
    cur = pl.BlockSpec((TILE, d), lambda g: (g, 0))
    nxt = pl.BlockSpec((SUB, d), lambda g: (jnp.minimum(2 * g + 2, last_group), 0))
    return cur, nxt


def _mixer_call(x2d, gmix, win, c4w, c4b, wr, br, wi, bi, lam, c3w, wout, *, seq_len):
    n, d = x2d.shape
    w_in_cols = win.shape[1]
    f32 = jnp.float32
    cur, nxt = _row_specs(n, d)
    consts = (gmix, win, c4w, c4b, wr, br, wi, bi, lam, c3w, wout)
    return pl.pallas_call(
        functools.partial(_mixer_kernel, groups_per_seq=seq_len // SUB, d=d),
        grid=(n // TILE,),
        in_specs=[cur, nxt] + [_resident(c.shape) for c in consts],
        out_specs=cur,
        out_shape=jax.ShapeDtypeStruct((n, d), f32),
        scratch_shapes=[
            pltpu.VMEM((SUB, w_in_cols), f32),
            pltpu.VMEM((SUB, w_in_cols), f32),
            pltpu.VMEM((SUB, d), f32),
            pltpu.VMEM((SUB + _H3 * SUBLANES, d), f32),
            pltpu.VMEM((SUB + _H3 * SUBLANES, d), f32),
            pltpu.VMEM((SUB, d), f32),
            pltpu.VMEM((SUB, d), f32),
            pltpu.VMEM((SUB, d), f32),
            pltpu.VMEM((SUB, d), f32),
            pltpu.VMEM((SUB, d), f32),
            pltpu.VMEM((_H4 * SUBLANES, d), f32),
            pltpu.VMEM((_H3 * SUBLANES, d), f32),
            pltpu.VMEM((SUBLANES, d), f32),
            pltpu.VMEM((SUBLANES, d), f32),
        ],
        compiler_params=pltpu.CompilerParams(
            dimension_semantics=("arbitrary",), vmem_limit_bytes=VMEM_LIMIT_BYTES),
        name="mixer",
    )(x2d, x2d, *consts)


def _ffn_call(x2d, p2d, gffn, wgu, wd, gple, wpg, wple, gfin, *, final):
    n, d = x2d.shape
    d_ff = wgu.shape[1] // 2
    cur, nxt = _row_specs(n, d)
    p_spec = pl.BlockSpec((TILE, p2d.shape[1]), lambda g: (g, 0))
    consts = (gffn, wgu, wd, gple, wpg, wple, gfin)
    return pl.pallas_call(
        functools.partial(_ffn_kernel, d_ff=d_ff, final=final),
        grid=(n // TILE,),
        in_specs=[cur, nxt, p_spec] + [_resident(c.shape) for c in consts],
        out_specs=cur,
        out_shape=jax.ShapeDtypeStruct((n, d), jnp.float32),
        scratch_shapes=[pltpu.VMEM((SUB, 2 * d_ff), jnp.float32),
                        pltpu.VMEM((SUB, d_ff), jnp.bfloat16),
                        pltpu.VMEM((SUB, d_ff), jnp.bfloat16)],
        compiler_params=pltpu.CompilerParams(
            dimension_semantics=("arbitrary",), vmem_limit_bytes=VMEM_LIMIT_BYTES),
        name="ffn_ple",
    )(x2d, x2d, p2d, *consts)


def kernel(x, p, g_mix, w_in, conv4_w, conv4_b, w_rg_r, b_rg_r, w_rg_i, b_rg_i, lru_lambda, conv3_w, w_out, g_ffn, w_gate_up, w_down, g_ple, w_ple_gate, w_ple, g_final):
    bsz, seq_len, d = x.shape
    depth = p.shape[0]
    assert seq_len % TILE == 0 and d % (RG_HEADS * 128) == 0
    pack = _pack_weights
    row = lambda v: v.reshape(1, -1)
    rows8 = lambda v: jnp.repeat(v.reshape(-1, v.shape[-1]), SUBLANES, axis=0)

    xs = _permute_rows(x.reshape(bsz * seq_len, d))
    for i in range(depth):
        xs = _mixer_call(
            xs, row(g_mix[i]), pack(w_in[i]), rows8(conv4_w[i]), rows8(conv4_b[i]),
            pack(w_rg_r[i]), rows8(b_rg_r[i].reshape(-1)), pack(w_rg_i[i]),
            rows8(b_rg_i[i].reshape(-1)), rows8(lru_lambda[i]), rows8(conv3_w[i]),
            pack(w_out[i]), seq_len=seq_len)
        p_i = _permute_rows(p[i].reshape(bsz * seq_len, -1))
        xs = _ffn_call(
            xs, p_i, row(g_ffn[i]), pack(w_gate_up[i]), pack(w_down[i]),
            row(g_ple[i]), pack(w_ple_gate[i]), pack(w_ple[i]),
            row(g_final), final=(i == depth - 1))
    return _unpermute_rows(xs).reshape(bsz, seq_len, d)
```

```python
import functools

import jax
import jax.numpy as jnp
from jax import lax
from jax.experimental import pallas as pl
from jax.experimental.pallas import tpu as pltpu

EPS = 1e-6
RG_C = 8.0
RG_HEADS = 4
CONV4_WIDTH = 4
CONV3_WIDTH = 3

SUBLANES = 8
LANES = 128
PACK_BLOCK_ROWS = 512
PACK_BLOCK_COLS = 2048
SUB = 128
J = SUB // SUBLANES
TILE = 2 * SUB
VMEM_LIMIT_BYTES = 60000 * 1024

_H3 = CONV3_WIDTH - 1
_H4 = CONV4_WIDTH - 1


def _permute_rows(a):
    n, c = a.shape
    return a.reshape(n // SUB, SUBLANES, J, c).transpose(0, 2, 1, 3).reshape(n, c)


def _unpermute_rows(a):
    n, c = a.shape
    return a.reshape(n // SUB, J, SUBLANES, c).transpose(0, 2, 1, 3).reshape(n, c)


def _rmsnorm(xs, g):
    ms = jnp.mean(xs * xs, axis=-1, keepdims=True)
    return (xs * lax.rsqrt(ms + EPS)) * g


def _softplus(z):
    return jnp.maximum(z, 0.0) + jnp.log1p(jnp.exp(-jnp.abs(z)))


def _rows(j, n=1):
    return slice(j * SUBLANES, (j + n) * SUBLANES)


def _largest_block(size, unit, cap):
    return max(b for b in range(unit, min(size, cap) + 1, unit) if size % b == 0)


def _pack_kernel(w_ref, o_ref):
    o_ref[...] = pltpu.bitcast(w_ref[...].astype(jnp.bfloat16), jnp.uint32)


def _pack_weights(w):
    layers, k, n = w.shape
    bk = _largest_block(k, 2 * SUBLANES, PACK_BLOCK_ROWS)
    bn = _largest_block(n, LANES, PACK_BLOCK_COLS)
    return pl.pallas_call(
        _pack_kernel,
        grid=(layers, k // bk, n // bn),
        in_specs=[pl.BlockSpec((None, bk, bn), lambda l, i, j: (l, i, j))],
        out_specs=pl.BlockSpec((None, bk // 2, bn), lambda l, i, j: (l, i, j)),
        out_shape=jax.ShapeDtypeStruct((layers, k // 2, n), jnp.uint32),
        name="pack_weights",
    )(w)


def _weights(packed):
    return pltpu.bitcast(packed, jnp.bfloat16)


def _mixer_kernel(x_ref, xn_ref, gmix_ref, win_ref, c4w_ref, c4b_ref, wr_ref, br_ref,
                  wi_ref, bi_ref, lam_ref, c3w_ref, wout_ref, o_ref,
                  z0_ref, z1_ref, xc_ref, q0_ref, q1_ref,
                  a_ref, u_ref, hl_ref, pl_ref, mrg_ref,
                  p4_ref, pq_ref, hc_ref, c_ref, *, groups_per_seq, d):
    g = pl.program_id(0)
    hb = d // RG_HEADS
    o_rx, o_ry, o_cb, o_cc, o_cx, o_gr, o_gc = (k * d for k in range(7))
    first_stream = lax.broadcasted_iota(jnp.int32, (SUBLANES, d), 0) == 0

    def from_prev_stream(cur, prev):
        return jnp.where(first_stream, pltpu.roll(prev, 1, 0), pltpu.roll(cur, 1, 0))

    def project(xs, z_ref):
        hn = _rmsnorm(xs, gmix_ref[...]).astype(jnp.bfloat16)
        z_ref[...] = jnp.dot(hn, _weights(win_ref[...]), preferred_element_type=jnp.float32)

    def convs(z_ref, q_ref, seq_start):
        def history(h_ref, jj):
            return jnp.where(seq_start, 0.0, h_ref[_rows(jj), :])

        halo4 = [from_prev_stream(z_ref[_rows(J - _H4 + jj), o_rx:o_rx + d],
                                  history(p4_ref, jj)) for jj in range(_H4)]
        p4_ref[...] = z_ref[_rows(J - _H4, _H4), o_rx:o_rx + d]

        def rx(j):
            return halo4[j + _H4] if j < 0 else z_ref[_rows(j), o_rx:o_rx + d]

        for j in range(J):
            acc = c4b_ref[...] + c4w_ref[_rows(0), :] * rx(j - _H4)
            for k in range(1, CONV4_WIDTH):
                acc = acc + c4w_ref[_rows(k), :] * rx(j - _H4 + k)
            xc_ref[_rows(j), :] = acc

        for j in range(J):
            q_ref[_rows(_H3 + j), :] = (z_ref[_rows(j), o_cc:o_cc + d]
                                        * z_ref[_rows(j), o_cx:o_cx + d])
        for jj in range(_H3):
            q_ref[_rows(jj), :] = from_prev_stream(q_ref[_rows(J + jj), :],
                                                   history(pq_ref, jj))
        pq_ref[...] = q_ref[_rows(J, _H3), :]

    def gates():
        for h in range(RG_HEADS):
            cols = slice(h * hb, (h + 1) * hb)
            xh = xc_ref[:, cols]
            xb = xh.astype(jnp.bfloat16)
            r_pre = jnp.dot(xb, _weights(wr_ref[h]), preferred_element_type=jnp.float32)
            i_pre = jnp.dot(xb, _weights(wi_ref[h]), preferred_element_type=jnp.float32)
            log_a_scale = -RG_C * _softplus(-lam_ref[:, cols])
            for j in range(J):
                r = jax.nn.sigmoid(r_pre[_rows(j), :] + br_ref[:, cols])
                gi = jax.nn.sigmoid(i_pre[_rows(j), :] + bi_ref[:, cols])
                log_a = log_a_scale * r
                a = jnp.exp(log_a)
                mult = jnp.sqrt(-jnp.tanh(log_a) * (1.0 + a * a))
                a_ref[_rows(j), cols] = a
                u_ref[_rows(j), cols] = mult * (gi * xh[_rows(j), :])

    def scan_merge(z_ref, q_ref, seq_start):
        hloc = jnp.zeros((SUBLANES, d), jnp.float32)
        prod = jnp.ones((SUBLANES, d), jnp.float32)
        for j in range(J):
            a = a_ref[_rows(j), :]
            hloc = a * hloc + u_ref[_rows(j), :]
            prod = a * prod
            hl_ref[_rows(j), :] = hloc
            pl_ref[_rows(j), :] = prod

        cur = jnp.where(seq_start, 0.0, hc_ref[0:1, :])
        last = (J - 1) * SUBLANES
        for s in range(SUBLANES):
            c_ref[s:s + 1, :] = cur
            cur = hl_ref[last + s:last + s + 1, :] + pl_ref[last + s:last + s + 1, :] * cur
        hc_ref[0:1, :] = cur
        cvec = c_ref[...]

        for j in range(J):
            hfull = hl_ref[_rows(j), :] + pl_ref[_rows(j), :] * cvec
            y_rnn = jax.nn.gelu(z_ref[_rows(j), o_ry:o_ry + d]) * hfull
            yc = c3w_ref[_rows(0), :] * q_ref[_rows(j), :]
            for k in range(1, CONV3_WIDTH):
                yc = yc + c3w_ref[_rows(k), :] * q_ref[_rows(j + k), :]
            y_conv = z_ref[_rows(j), o_cb:o_cb + d] * yc
            mrg_ref[_rows(j), :] = (
                jax.nn.sigmoid(z_ref[_rows(j), o_gr:o_gr + d]) * y_rnn
                + jax.nn.sigmoid(z_ref[_rows(j), o_gc:o_gc + d]) * y_conv)

    def output(xs):
        return xs + jnp.dot(mrg_ref[...].astype(jnp.bfloat16), _weights(wout_ref[...]),
                            preferred_element_type=jnp.float32)

    group0 = 2 * g
    starts = [(group0 + k) % groups_per_seq == 0 for k in range(3)]

    @pl.when(g == 0)
    def _():
        p4_ref[...] = jnp.zeros_like(p4_ref)
        pq_ref[...] = jnp.zeros_like(pq_ref)
        hc_ref[...] = jnp.zeros_like(hc_ref)
        project(x_ref[0:SUB, :], z0_ref)
        convs(z0_ref, q0_ref, True)

    gates()
    project(x_ref[SUB:TILE, :], z1_ref)
    scan_merge(z0_ref, q0_ref, starts[0])
    o_ref[0:SUB, :] = output(x_ref[0:SUB, :])
    convs(z1_ref, q1_ref, starts[1])

    gates()
    project(xn_ref[...], z0_ref)
    scan_merge(z1_ref, q1_ref, starts[1])
    o_ref[SUB:TILE, :] = output(x_ref[SUB:TILE, :])
    convs(z0_ref, q0_ref, starts[2])


def _ffn_kernel(x_ref, xn_ref, p_ref, gffn_ref, wgu_ref, wd_ref, gple_ref, wpg_ref,
                wple_ref, gfin_ref, o_ref, gu_ref, act0_ref, act1_ref, *, d_ff, final):
    g = pl.program_id(0)

    def project(xs):
        hn = _rmsnorm(xs, gffn_ref[...]).astype(jnp.bfloat16)
        gu_ref[...] = jnp.dot(hn, _weights(wgu_ref[...]), preferred_element_type=jnp.float32)

    def activate(act_ref):
        act_ref[...] = (jax.nn.silu(gu_ref[:, 0:d_ff])
                        * gu_ref[:, d_ff:2 * d_ff]).astype(jnp.bfloat16)

    def down(act_ref, xs):
        return xs + jnp.dot(act_ref[...], _weights(wd_ref[...]), preferred_element_type=jnp.float32)

    def embed(x2, ps):
        hn2 = _rmsnorm(x2, gple_ref[...]).astype(jnp.bfloat16)
        gate = jax.nn.sigmoid(jnp.dot(hn2, _weights(wpg_ref[...]), preferred_element_type=jnp.float32))
        pe = jnp.dot(ps.astype(jnp.bfloat16), _weights(wple_ref[...]), preferred_element_type=jnp.float32)
        x3 = x2 + gate * pe
        if final:
            x3 = _rmsnorm(x3, gfin_ref[...])
        return x3

    @pl.when(g == 0)
    def _():
        project(x_ref[0:SUB, :])
        activate(act0_ref)

    x2 = down(act0_ref, x_ref[0:SUB, :])
    project(x_ref[SUB:TILE, :])
    activate(act1_ref)
    o_ref[0:SUB, :] = embed(x2, p_ref[0:SUB, :])

    x2 = down(act1_ref, x_ref[SUB:TILE, :])
    project(xn_ref[...])
    activate(act0_ref)
    o_ref[SUB:TILE, :] = embed(x2, p_ref[SUB:TILE, :])


def _resident(shape):
    zeros = (0,) * len(shape)
    return pl.BlockSpec(shape, lambda g: zeros, pipeline_mode=pl.Buffered(1))


def _resident_layer(stacked_shape, layer):
    index = (layer,) + (0,) * (len(stacked_shape) - 1)
    return pl.BlockSpec((None,) + tuple(stacked_shape[1:]), lambda g: index,
                        pipeline_mode=pl.Buffered(1))


def _row_specs(n, d):
    last_group = n // SUB - 1
    cur = pl.BlockSpec((TILE, d), lambda g: (g, 0))
    nxt = pl.BlockSpec((SUB, d), lambda g: (jnp.minimum(2 * g + 2, last_group), 0))
    return cur, nxt


def _mixer_call(x2d, gmix, win, c4w, c4b, wr, br, wi, bi, lam, c3w, wout, *, layer, seq_len):
    n, d = x2d.shape
    w_in_cols = win.shape[-1]
    f32 = jnp.float32
    cur, nxt = _row_specs(n, d)
    consts = (gmix, win, c4w, c4b, wr, br, wi, bi, lam, c3w, wout)
    stacked = (win, wr, wi, wout)
    return pl.pallas_call(
        functools.partial(_mixer_kernel, groups_per_seq=seq_len // SUB, d=d),
        grid=(n // TILE,),
        in_specs=[cur, nxt] + [
            _resident_layer(c.shape, layer) if any(c is s for s in stacked)
            else _resident(c.shape) for c in consts],
        out_specs=cur,
        out_shape=jax.ShapeDtypeStruct((n, d), f32),
        scratch_shapes=[
            pltpu.VMEM((SUB, w_in_cols), f32),
            pltpu.VMEM((SUB, w_in_cols), f32),
            pltpu.VMEM((SUB, d), f32),
            pltpu.VMEM((SUB + _H3 * SUBLANES, d), f32),
            pltpu.VMEM((SUB + _H3 * SUBLANES, d), f32),
            pltpu.VMEM((SUB, d), f32),
            pltpu.VMEM((SUB, d), f32),
            pltpu.VMEM((SUB, d), f32),
            pltpu.VMEM((SUB, d), f32),
            pltpu.VMEM((SUB, d), f32),
            pltpu.VMEM((_H4 * SUBLANES, d), f32),
            pltpu.VMEM((_H3 * SUBLANES, d), f32),
            pltpu.VMEM((SUBLANES, d), f32),
            pltpu.VMEM((SUBLANES, d), f32),
        ],
        compiler_params=pltpu.CompilerParams(
            dimension_semantics=("arbitrary",), vmem_limit_bytes=VMEM_LIMIT_BYTES),
        name="mixer",
    )(x2d, x2d, *consts)


def _ffn_call(x2d, p2d, gffn, wgu, wd, gple, wpg, wple, gfin, *, layer, final):
    n, d = x2d.shape
    d_ff = wgu.shape[-1] // 2
    cur, nxt = _row_specs(n, d)
    p_spec = pl.BlockSpec((TILE, p2d.shape[1]), lambda g: (g, 0))
    consts = (gffn, wgu, wd, gple, wpg, wple, gfin)
    stacked = (wgu, wd, wpg, wple)
    return pl.pallas_call(
        functools.partial(_ffn_kernel, d_ff=d_ff, final=final),
        grid=(n // TILE,),
        in_specs=[cur, nxt, p_spec] + [
            _resident_layer(c.shape, layer) if any(c is s for s in stacked)
            else _resident(c.shape) for c in consts],
        out_specs=cur,
        out_shape=jax.ShapeDtypeStruct((n, d), jnp.float32),
        scratch_shapes=[pltpu.VMEM((SUB, 2 * d_ff), jnp.float32),
                        pltpu.VMEM((SUB, d_ff), jnp.bfloat16),
                        pltpu.VMEM((SUB, d_ff), jnp.bfloat16)],
        compiler_params=pltpu.CompilerParams(
            dimension_semantics=("arbitrary",), vmem_limit_bytes=VMEM_LIMIT_BYTES),
        name="ffn_ple",
    )(x2d, x2d, p2d, *consts)


def kernel(x, p, g_mix, w_in, conv4_w, conv4_b, w_rg_r, b_rg_r, w_rg_i, b_rg_i, lru_lambda, conv3_w, w_out, g_ffn, w_gate_up, w_down, g_ple, w_ple_gate, w_ple, g_final):
    bsz, seq_len, d = x.shape
    depth = p.shape[0]
    assert seq_len % TILE == 0 and d % (RG_HEADS * 128) == 0
    row = lambda v: v.reshape(1, -1)
    rows8 = lambda v: jnp.repeat(v.reshape(-1, v.shape[-1]), SUBLANES, axis=0)

    def pack_heads(w):
        hb = w.shape[-1]
        return _pack_weights(w.reshape(-1, hb, hb)).reshape(depth, -1, hb // 2, hb)

    win_p, wout_p = _pack_weights(w_in), _pack_weights(w_out)
    wr_p, wi_p = pack_heads(w_rg_r), pack_heads(w_rg_i)
    wgu_p, wd_p = _pack_weights(w_gate_up), _pack_weights(w_down)
    wpg_p, wple_p = _pack_weights(w_ple_gate), _pack_weights(w_ple)

    xs = _permute_rows(x.reshape(bsz * seq_len, d))
    for i in range(depth):
        xs = _mixer_call(
            xs, row(g_mix[i]), win_p, rows8(conv4_w[i]), rows8(conv4_b[i]),
            wr_p, rows8(b_rg_r[i].reshape(-1)), wi_p,
            rows8(b_rg_i[i].reshape(-1)), rows8(lru_lambda[i]), rows8(conv3_w[i]),
            wout_p, layer=i, seq_len=seq_len)
        p_i = _permute_rows(p[i].reshape(bsz * seq_len, -1))
        xs = _ffn_call(
            xs, p_i, row(g_ffn[i]), wgu_p, wd_p,
            row(g_ple[i]), wpg_p, wple_p,
            row(g_final), layer=i, final=(i == depth - 1))
    return _unpermute_rows(xs).reshape(bsz, seq_len, d)
```

```python
import functools

import jax
import jax.numpy as jnp
from jax import lax
from jax.experimental import pallas as pl
from jax.experimental.pallas import tpu as pltpu

EPS = 1e-6
RG_C = 8.0
RG_HEADS = 4
CONV4_WIDTH = 4
CONV3_WIDTH = 3

SUBLANES = 8
LANES = 128
PACK_BLOCK_ROWS = 512
PACK_BLOCK_COLS = 2048
SUB = 128
J = SUB // SUBLANES
TILE = 2 * SUB
VMEM_LIMIT_BYTES = 60000 * 1024

_H3 = CONV3_WIDTH - 1
_H4 = CONV4_WIDTH - 1


def _permute_rows(a):
    n, c = a.shape
    return a.reshape(n // SUB, SUBLANES, J, c).transpose(0, 2, 1, 3).reshape(n, c)


def _unpermute_rows(a):
    n, c = a.shape
    return a.reshape(n // SUB, J, SUBLANES, c).transpose(0, 2, 1, 3).reshape(n, c)


def _rmsnorm(xs, g):
    ms = jnp.mean(xs * xs, axis=-1, keepdims=True)
    return (xs * lax.rsqrt(ms + EPS)) * g


def _sigmoid(z):
    return 0.5 * jnp.tanh(0.5 * z) + 0.5


def _softplus(z):
    return jnp.maximum(z, 0.0) + jnp.log1p(jnp.exp(-jnp.abs(z)))


def _rows(j, n=1):
    return slice(j * SUBLANES, (j + n) * SUBLANES)


def _largest_block(size, unit, cap):
    return max(b for b in range(unit, min(size, cap) + 1, unit) if size % b == 0)


def _pack_kernel(w_ref, o_ref):
    o_ref[...] = pltpu.bitcast(w_ref[...].astype(jnp.bfloat16), jnp.uint32)


def _pack_weights(w):
    layers, k, n = w.shape
    bk = _largest_block(k, 2 * SUBLANES, PACK_BLOCK_ROWS)
    bn = _largest_block(n, LANES, PACK_BLOCK_COLS)
    return pl.pallas_call(
        _pack_kernel,
        grid=(layers, k // bk, n // bn),
        in_specs=[pl.BlockSpec((None, bk, bn), lambda l, i, j: (l, i, j))],
        out_specs=pl.BlockSpec((None, bk // 2, bn), lambda l, i, j: (l, i, j)),
        out_shape=jax.ShapeDtypeStruct((layers, k // 2, n), jnp.uint32),
        name="pack_weights",
    )(w)


def _weights(packed):
    return pltpu.bitcast(packed, jnp.bfloat16)


def _mixer_kernel(x_ref, xn_ref, gmix_ref, win_ref, c4w_ref, c4b_ref, wr_ref, br_ref,
                  wi_ref, bi_ref, lam_ref, c3w_ref, wout_ref, o_ref,
                  z0_ref, z1_ref, xc_ref, q0_ref, q1_ref,
                  rp_ref, ip_ref, a_ref, u_ref, hl_ref, pl_ref, mrg_ref,
                  p4_ref, pq_ref, hc_ref, c_ref, *, groups_per_seq, d):
    g = pl.program_id(0)
    hb = d // RG_HEADS
    o_rx, o_ry, o_cb, o_cc, o_cx, o_gr, o_gc = (k * d for k in range(7))
    first_stream = lax.broadcasted_iota(jnp.int32, (SUBLANES, d), 0) == 0

    def from_prev_stream(cur, prev):
        return jnp.where(first_stream, pltpu.roll(prev, 1, 0), pltpu.roll(cur, 1, 0))

    def project(xs, z_ref):
        hn = _rmsnorm(xs, gmix_ref[...]).astype(jnp.bfloat16)
        z_ref[...] = jnp.dot(hn, _weights(win_ref[...]), preferred_element_type=jnp.float32)

    def convs(z_ref, q_ref, seq_start):
        def history(h_ref, jj):
            return jnp.where(seq_start, 0.0, h_ref[_rows(jj), :])

        halo4 = [from_prev_stream(z_ref[_rows(J - _H4 + jj), o_rx:o_rx + d],
                                  history(p4_ref, jj)) for jj in range(_H4)]
        p4_ref[...] = z_ref[_rows(J - _H4, _H4), o_rx:o_rx + d]

        def rx(j):
            return halo4[j + _H4] if j < 0 else z_ref[_rows(j), o_rx:o_rx + d]

        for j in range(J):
            acc = c4b_ref[...] + c4w_ref[_rows(0), :] * rx(j - _H4)
            for k in range(1, CONV4_WIDTH):
                acc = acc + c4w_ref[_rows(k), :] * rx(j - _H4 + k)
            xc_ref[_rows(j), :] = acc

        for j in range(J):
            q_ref[_rows(_H3 + j), :] = (z_ref[_rows(j), o_cc:o_cc + d]
                                        * z_ref[_rows(j), o_cx:o_cx + d])
        for jj in range(_H3):
            q_ref[_rows(jj), :] = from_prev_stream(q_ref[_rows(J + jj), :],
                                                   history(pq_ref, jj))
        pq_ref[...] = q_ref[_rows(J, _H3), :]

    def gates():
        for h in range(RG_HEADS):
            cols = slice(h * hb, (h + 1) * hb)
            xh = xc_ref[:, cols]
            xb = xh.astype(jnp.bfloat16)
            slot = lax.rem(g, 2)
            rp_ref[slot, :, cols] = jnp.dot(xb, _weights(wr_ref[h]), preferred_element_type=jnp.float32)
            ip_ref[slot, :, cols] = jnp.dot(xb, _weights(wi_ref[h]), preferred_element_type=jnp.float32)
        for h in range(RG_HEADS):
            cols = slice(h * hb, (h + 1) * hb)
            xh = xc_ref[:, cols]
            log_a_scale = -RG_C * _softplus(-lam_ref[:, cols])
            for j in range(J):
                r = _sigmoid(rp_ref[slot, _rows(j), cols] + br_ref[:, cols])
                gi = _sigmoid(ip_ref[slot, _rows(j), cols] + bi_ref[:, cols])
                log_a = log_a_scale * r
                a = jnp.exp(log_a)
                mult = jnp.sqrt(-jnp.tanh(log_a) * (1.0 + a * a))
                a_ref[_rows(j), cols] = a
                u_ref[_rows(j), cols] = mult * (gi * xh[_rows(j), :])

    def scan_merge(z_ref, q_ref, seq_start):
        hloc = jnp.zeros((SUBLANES, d), jnp.float32)
        prod = jnp.ones((SUBLANES, d), jnp.float32)
        for j in range(J):
            a = a_ref[_rows(j), :]
            hloc = a * hloc + u_ref[_rows(j), :]
            prod = a * prod
            hl_ref[_rows(j), :] = hloc
            pl_ref[_rows(j), :] = prod

        cur = jnp.where(seq_start, 0.0, hc_ref[0:1, :])
        last = (J - 1) * SUBLANES
        for s in range(SUBLANES):
            c_ref[s:s + 1, :] = cur
            cur = hl_ref[last + s:last + s + 1, :] + pl_ref[last + s:last + s + 1, :] * cur
        hc_ref[0:1, :] = cur
        cvec = c_ref[...]

        for j in range(J):
            hfull = hl_ref[_rows(j), :] + pl_ref[_rows(j), :] * cvec
            y_rnn = jax.nn.gelu(z_ref[_rows(j), o_ry:o_ry + d]) * hfull
            yc = c3w_ref[_rows(0), :] * q_ref[_rows(j), :]
            for k in range(1, CONV3_WIDTH):
                yc = yc + c3w_ref[_rows(k), :] * q_ref[_rows(j + k), :]
            y_conv = z_ref[_rows(j), o_cb:o_cb + d] * yc
            mrg_ref[_rows(j), :] = (
                _sigmoid(z_ref[_rows(j), o_gr:o_gr + d]) * y_rnn
                + _sigmoid(z_ref[_rows(j), o_gc:o_gc + d]) * y_conv)

    def output(xs):
        return xs + jnp.dot(mrg_ref[...].astype(jnp.bfloat16), _weights(wout_ref[...]),
                            preferred_element_type=jnp.float32)

    group0 = 2 * g
    starts = [(group0 + k) % groups_per_seq == 0 for k in range(3)]

    @pl.when(g == 0)
    def _():
        p4_ref[...] = jnp.zeros_like(p4_ref)
        pq_ref[...] = jnp.zeros_like(pq_ref)
        hc_ref[...] = jnp.zeros_like(hc_ref)
        project(x_ref[0:SUB, :], z0_ref)
        convs(z0_ref, q0_ref, True)

    gates()
    project(x_ref[SUB:TILE, :], z1_ref)
    scan_merge(z0_ref, q0_ref, starts[0])
    o_ref[0:SUB, :] = output(x_ref[0:SUB, :])
    convs(z1_ref, q1_ref, starts[1])

    gates()
    project(xn_ref[...], z0_ref)
    scan_merge(z1_ref, q1_ref, starts[1])
    o_ref[SUB:TILE, :] = output(x_ref[SUB:TILE, :])
    convs(z0_ref, q0_ref, starts[2])


def _ffn_kernel(x_ref, xn_ref, p_ref, gffn_ref, wgu_ref, wd_ref, gple_ref, wpg_ref,
                wple_ref, gfin_ref, o_ref, gu_ref, act0_ref, act1_ref, *, d_ff, final):
    g = pl.program_id(0)

    def project(xs):
        hn = _rmsnorm(xs, gffn_ref[...]).astype(jnp.bfloat16)
        gu_ref[...] = jnp.dot(hn, _weights(wgu_ref[...]), preferred_element_type=jnp.float32)

    def activate(act_ref):
        gate = gu_ref[:, 0:d_ff]
        act_ref[...] = ((gate * _sigmoid(gate)) * gu_ref[:, d_ff:2 * d_ff]).astype(jnp.bfloat16)

    def down(act_ref, xs):
        return xs + jnp.dot(act_ref[...], _weights(wd_ref[...]), preferred_element_type=jnp.float32)

    def embed(x2, ps):
        hn2 = _rmsnorm(x2, gple_ref[...]).astype(jnp.bfloat16)
        gate = _sigmoid(jnp.dot(hn2, _weights(wpg_ref[...]), preferred_element_type=jnp.float32))
        pe = jnp.dot(ps.astype(jnp.bfloat16), _weights(wple_ref[...]), preferred_element_type=jnp.float32)
        x3 = x2 + gate * pe
        if final:
            x3 = _rmsnorm(x3, gfin_ref[...])
        return x3

    @pl.when(g == 0)
    def _():
        project(x_ref[0:SUB, :])
        activate(act0_ref)

    x2 = down(act0_ref, x_ref[0:SUB, :])
    project(x_ref[SUB:TILE, :])
    activate(act1_ref)
    o_ref[0:SUB, :] = embed(x2, p_ref[0:SUB, :])

    x2 = down(act1_ref, x_ref[SUB:TILE, :])
    project(xn_ref[...])
    activate(act0_ref)
    o_ref[SUB:TILE, :] = embed(x2, p_ref[SUB:TILE, :])


def _resident(shape):
    zeros = (0,) * len(shape)
    return pl.BlockSpec(shape, lambda g: zeros, pipeline_mode=pl.Buffered(1))


def _resident_layer(stacked_shape, layer):
    index = (layer,) + (0,) * (len(stacked_shape) - 1)
    return pl.BlockSpec((None,) + tuple(stacked_shape[1:]), lambda g: index,
                        pipeline_mode=pl.Buffered(1))


def _row_specs(n, d):
    last_group = n // SUB - 1
    cur = pl.BlockSpec((TILE, d), lambda g: (g, 0))
    nxt = pl.BlockSpec((SUB, d), lambda g: (jnp.minimum(2 * g + 2, last_group), 0))
    return cur, nxt


def _mixer_call(x2d, gmix, win, c4w, c4b, wr, br, wi, bi, lam, c3w, wout, *, layer, seq_len):
    n, d = x2d.shape
    w_in_cols = win.shape[-1]
    f32 = jnp.float32
    cur, nxt = _row_specs(n, d)
    consts = (gmix, win, c4w, c4b, wr, br, wi, bi, lam, c3w, wout)
    stacked = (win, wr, wi, wout)
    return pl.pallas_call(
        functools.partial(_mixer_kernel, groups_per_seq=seq_len // SUB, d=d),
        grid=(n // TILE,),
        in_specs=[cur, nxt] + [
            _resident_layer(c.shape, layer) if any(c is s for s in stacked)
            else _resident(c.shape) for c in consts],
        out_specs=cur,
        out_shape=jax.ShapeDtypeStruct((n, d), f32),
        scratch_shapes=[
            pltpu.VMEM((SUB, w_in_cols), f32),
            pltpu.VMEM((SUB, w_in_cols), f32),
            pltpu.VMEM((SUB, d), f32),
            pltpu.VMEM((SUB + _H3 * SUBLANES, d), f32),
            pltpu.VMEM((SUB + _H3 * SUBLANES, d), f32),
            pltpu.VMEM((2, SUB, d), f32),
            pltpu.VMEM((2, SUB, d), f32),
            pltpu.VMEM((SUB, d), f32),
            pltpu.VMEM((SUB, d), f32),
            pltpu.VMEM((SUB, d), f32),
            pltpu.VMEM((SUB, d), f32),
            pltpu.VMEM((SUB, d), f32),
            pltpu.VMEM((_H4 * SUBLANES, d), f32),
            pltpu.VMEM((_H3 * SUBLANES, d), f32),
            pltpu.VMEM((SUBLANES, d), f32),
            pltpu.VMEM((SUBLANES, d), f32),
        ],
        compiler_params=pltpu.CompilerParams(
            dimension_semantics=("arbitrary",), vmem_limit_bytes=VMEM_LIMIT_BYTES),
        name="mixer",
    )(x2d, x2d, *consts)


def _ffn_call(x2d, p2d, gffn, wgu, wd, gple, wpg, wple, gfin, *, layer, final):
    n, d = x2d.shape
    d_ff = wgu.shape[-1] // 2
    cur, nxt = _row_specs(n, d)
    p_spec = pl.BlockSpec((TILE, p2d.shape[1]), lambda g: (g, 0))
    consts = (gffn, wgu, wd, gple, wpg, wple, gfin)
    stacked = (wgu, wd, wpg, wple)
    return pl.pallas_call(
        functools.partial(_ffn_kernel, d_ff=d_ff, final=final),
        grid=(n // TILE,),
        in_specs=[cur, nxt, p_spec] + [
            _resident_layer(c.shape, layer) if any(c is s for s in stacked)
            else _resident(c.shape) for c in consts],
        out_specs=cur,
        out_shape=jax.ShapeDtypeStruct((n, d), jnp.float32),
        scratch_shapes=[pltpu.VMEM((SUB, 2 * d_ff), jnp.float32),
                        pltpu.VMEM((SUB, d_ff), jnp.bfloat16),
                        pltpu.VMEM((SUB, d_ff), jnp.bfloat16)],
        compiler_params=pltpu.CompilerParams(
            dimension_semantics=("arbitrary",), vmem_limit_bytes=VMEM_LIMIT_BYTES),
        name="ffn_ple",
    )(x2d, x2d, p2d, *consts)


def kernel(x, p, g_mix, w_in, conv4_w, conv4_b, w_rg_r, b_rg_r, w_rg_i, b_rg_i, lru_lambda, conv3_w, w_out, g_ffn, w_gate_up, w_down, g_ple, w_ple_gate, w_ple, g_final):
    bsz, seq_len, d = x.shape
    depth = p.shape[0]
    assert seq_len % TILE == 0 and d % (RG_HEADS * 128) == 0
    row = lambda v: v.reshape(1, -1)
    rows8 = lambda v: jnp.repeat(v.reshape(-1, v.shape[-1]), SUBLANES, axis=0)

    def pack_heads(w):
        hb = w.shape[-1]
        return _pack_weights(w.reshape(-1, hb, hb)).reshape(depth, -1, hb // 2, hb)

    win_p, wout_p = _pack_weights(w_in), _pack_weights(w_out)
    wr_p, wi_p = pack_heads(w_rg_r), pack_heads(w_rg_i)
    wgu_p, wd_p = _pack_weights(w_gate_up), _pack_weights(w_down)
    wpg_p, wple_p = _pack_weights(w_ple_gate), _pack_weights(w_ple)

    xs = _permute_rows(x.reshape(bsz * seq_len, d))
    for i in range(depth):
        xs = _mixer_call(
            xs, row(g_mix[i]), win_p, rows8(conv4_w[i]), rows8(conv4_b[i]),
            wr_p, rows8(b_rg_r[i].reshape(-1)), wi_p,
            rows8(b_rg_i[i].reshape(-1)), rows8(lru_lambda[i]), rows8(conv3_w[i]),
            wout_p, layer=i, seq_len=seq_len)
        p_i = _permute_rows(p[i].reshape(bsz * seq_len, -1))
        xs = _ffn_call(
            xs, p_i, row(g_ffn[i]), wgu_p, wd_p,
            row(g_ple[i]), wpg_p, wple_p,
            row(g_final), layer=i, final=(i == depth - 1))
    return _unpermute_rows(xs).reshape(bsz, seq_len, d)
```

```python
import functools

import jax
import jax.numpy as jnp
from jax import lax
from jax.experimental import pallas as pl
from jax.experimental.pallas import tpu as pltpu

EPS = 1e-6
RG_C = 8.0
RG_HEADS = 4
CONV4_WIDTH = 4
CONV3_WIDTH = 3

SUBLANES = 8
LANES = 128
PACK_BLOCK_ROWS = 512
PACK_BLOCK_COLS = 2048
SUB = 128
J = SUB // SUBLANES
GROUPS = 4
TILE = GROUPS * SUB
VMEM_LIMIT_BYTES = 60000 * 1024

_H3 = CONV3_WIDTH - 1
_H4 = CONV4_WIDTH - 1


def _permute_rows(a):
    n, c = a.shape
    return a.reshape(n // SUB, SUBLANES, J, c).transpose(0, 2, 1, 3).reshape(n, c)


def _unpermute_rows(a):
    n, c = a.shape
    return a.reshape(n // SUB, J, SUBLANES, c).transpose(0, 2, 1, 3).reshape(n, c)


def _rmsnorm(xs, g):
    ms = jnp.mean(xs * xs, axis=-1, keepdims=True)
    return (xs * lax.rsqrt(ms + EPS)) * g


def _sigmoid(z):
    return 0.5 * jnp.tanh(0.5 * z) + 0.5


def _softplus(z):
    return jnp.maximum(z, 0.0) + jnp.log1p(jnp.exp(-jnp.abs(z)))


def _rows(j, n=1):
    return slice(j * SUBLANES, (j + n) * SUBLANES)


def _largest_block(size, unit, cap):
    return max(b for b in range(unit, min(size, cap) + 1, unit) if size % b == 0)


def _pack_kernel(w_ref, o_ref):
    o_ref[...] = pltpu.bitcast(w_ref[...].astype(jnp.bfloat16), jnp.uint32)


def _pack_weights(w):
    layers, k, n = w.shape
    bk = _largest_block(k, 2 * SUBLANES, PACK_BLOCK_ROWS)
    bn = _largest_block(n, LANES, PACK_BLOCK_COLS)
    return pl.pallas_call(
        _pack_kernel,
        grid=(layers, k // bk, n // bn),
        in_specs=[pl.BlockSpec((None, bk, bn), lambda l, i, j: (l, i, j))],
        out_specs=pl.BlockSpec((None, bk // 2, bn), lambda l, i, j: (l, i, j)),
        out_shape=jax.ShapeDtypeStruct((layers, k // 2, n), jnp.uint32),
        name="pack_weights",
    )(w)


def _weights(packed):
    return pltpu.bitcast(packed, jnp.bfloat16)


def _mixer_kernel(x_ref, xn_ref, gmix_ref, win_ref, c4w_ref, c4b_ref, wr_ref, br_ref,
                  wi_ref, bi_ref, lam_ref, c3w_ref, wout_ref, o_ref,
                  z0_ref, z1_ref, xc_ref, q0_ref, q1_ref,
                  rp_ref, ip_ref, a_ref, u_ref, hl_ref, pl_ref, mrg_ref,
                  p4_ref, pq_ref, hc_ref, c_ref, *, groups_per_seq, d):
    g = pl.program_id(0)
    hb = d // RG_HEADS
    o_rx, o_ry, o_cb, o_cc, o_cx, o_gr, o_gc = (k * d for k in range(7))
    first_stream = lax.broadcasted_iota(jnp.int32, (SUBLANES, d), 0) == 0

    def from_prev_stream(cur, prev):
        return jnp.where(first_stream, pltpu.roll(prev, 1, 0), pltpu.roll(cur, 1, 0))

    def project(xs, z_ref):
        hn = _rmsnorm(xs, gmix_ref[...]).astype(jnp.bfloat16)
        z_ref[...] = jnp.dot(hn, _weights(win_ref[...]), preferred_element_type=jnp.float32)

    def convs(z_ref, q_ref, seq_start):
        def history(h_ref, jj):
            return jnp.where(seq_start, 0.0, h_ref[_rows(jj), :])

        halo4 = [from_prev_stream(z_ref[_rows(J - _H4 + jj), o_rx:o_rx + d],
                                  history(p4_ref, jj)) for jj in range(_H4)]
        p4_ref[...] = z_ref[_rows(J - _H4, _H4), o_rx:o_rx + d]

        def rx(j):
            return halo4[j + _H4] if j < 0 else z_ref[_rows(j), o_rx:o_rx + d]

        for j in range(J):
            acc = c4b_ref[...] + c4w_ref[_rows(0), :] * rx(j - _H4)
            for k in range(1, CONV4_WIDTH):
                acc = acc + c4w_ref[_rows(k), :] * rx(j - _H4 + k)
            xc_ref[_rows(j), :] = acc

        for j in range(J):
            q_ref[_rows(_H3 + j), :] = (z_ref[_rows(j), o_cc:o_cc + d]
                                        * z_ref[_rows(j), o_cx:o_cx + d])
        for jj in range(_H3):
            q_ref[_rows(jj), :] = from_prev_stream(q_ref[_rows(J + jj), :],
                                                   history(pq_ref, jj))
        pq_ref[...] = q_ref[_rows(J, _H3), :]

    def gates():
        for h in range(RG_HEADS):
            cols = slice(h * hb, (h + 1) * hb)
            xh = xc_ref[:, cols]
            xb = xh.astype(jnp.bfloat16)
            slot = lax.rem(g, 2)
            rp_ref[slot, :, cols] = jnp.dot(xb, _weights(wr_ref[h]), preferred_element_type=jnp.float32)
            ip_ref[slot, :, cols] = jnp.dot(xb, _weights(wi_ref[h]), preferred_element_type=jnp.float32)
        for h in range(RG_HEADS):
            cols = slice(h * hb, (h + 1) * hb)
            xh = xc_ref[:, cols]
            log_a_scale = -RG_C * _softplus(-lam_ref[:, cols])
            for j in range(J):
                r = _sigmoid(rp_ref[slot, _rows(j), cols] + br_ref[:, cols])
                gi = _sigmoid(ip_ref[slot, _rows(j), cols] + bi_ref[:, cols])
                log_a = log_a_scale * r
                a = jnp.exp(log_a)
                mult = jnp.sqrt(-jnp.tanh(log_a) * (1.0 + a * a))
                a_ref[_rows(j), cols] = a
                u_ref[_rows(j), cols] = mult * (gi * xh[_rows(j), :])

    def scan_merge(z_ref, q_ref, seq_start):
        hloc = jnp.zeros((SUBLANES, d), jnp.float32)
        prod = jnp.ones((SUBLANES, d), jnp.float32)
        for j in range(J):
            a = a_ref[_rows(j), :]
            hloc = a * hloc + u_ref[_rows(j), :]
            prod = a * prod
            hl_ref[_rows(j), :] = hloc
            pl_ref[_rows(j), :] = prod

        cur = jnp.where(seq_start, 0.0, hc_ref[0:1, :])
        last = (J - 1) * SUBLANES
        for s in range(SUBLANES):
            c_ref[s:s + 1, :] = cur
            cur = hl_ref[last + s:last + s + 1, :] + pl_ref[last + s:last + s + 1, :] * cur
        hc_ref[0:1, :] = cur
        cvec = c_ref[...]

        for j in range(J):
            hfull = hl_ref[_rows(j), :] + pl_ref[_rows(j), :] * cvec
            y_rnn = jax.nn.gelu(z_ref[_rows(j), o_ry:o_ry + d]) * hfull
            yc = c3w_ref[_rows(0), :] * q_ref[_rows(j), :]
            for k in range(1, CONV3_WIDTH):
                yc = yc + c3w_ref[_rows(k), :] * q_ref[_rows(j + k), :]
            y_conv = z_ref[_rows(j), o_cb:o_cb + d] * yc
            mrg_ref[_rows(j), :] = (
                _sigmoid(z_ref[_rows(j), o_gr:o_gr + d]) * y_rnn
                + _sigmoid(z_ref[_rows(j), o_gc:o_gc + d]) * y_conv)

    def output(xs):
        return xs + jnp.dot(mrg_ref[...].astype(jnp.bfloat16), _weights(wout_ref[...]),
                            preferred_element_type=jnp.float32)

    starts = [(GROUPS * g + k) % groups_per_seq == 0 for k in range(GROUPS + 1)]
    z_slots, q_slots = (z0_ref, z1_ref), (q0_ref, q1_ref)

    @pl.when(g == 0)
    def _():
        p4_ref[...] = jnp.zeros_like(p4_ref)
        pq_ref[...] = jnp.zeros_like(pq_ref)
        hc_ref[...] = jnp.zeros_like(hc_ref)
        project(x_ref[0:SUB, :], z0_ref)
        convs(z0_ref, q0_ref, True)

    for k in range(GROUPS):
        rows = slice(k * SUB, (k + 1) * SUB)
        nxt = slice((k + 1) * SUB, (k + 2) * SUB)
        cur_slot, nxt_slot = k % 2, (k + 1) % 2
        gates()
        project(x_ref[nxt, :] if k + 1 < GROUPS else xn_ref[...], z_slots[nxt_slot])
        scan_merge(z_slots[cur_slot], q_slots[cur_slot], starts[k])
        o_ref[rows, :] = output(x_ref[rows, :])
        convs(z_slots[nxt_slot], q_slots[nxt_slot], starts[k + 1])


def _ffn_kernel(x_ref, xn_ref, p_ref, gffn_ref, wgu_ref, wd_ref, gple_ref, wpg_ref,
                wple_ref, gfin_ref, o_ref, gu_ref, act0_ref, act1_ref, *, d_ff, final):
    g = pl.program_id(0)

    def project(xs):
        hn = _rmsnorm(xs, gffn_ref[...]).astype(jnp.bfloat16)
        gu_ref[...] = jnp.dot(hn, _weights(wgu_ref[...]), preferred_element_type=jnp.float32)

    def activate(act_ref):
        gate = gu_ref[:, 0:d_ff]
        act_ref[...] = ((gate * _sigmoid(gate)) * gu_ref[:, d_ff:2 * d_ff]).astype(jnp.bfloat16)

    def down(act_ref, xs):
        return xs + jnp.dot(act_ref[...], _weights(wd_ref[...]), preferred_element_type=jnp.float32)

    def embed(x2, ps):
        hn2 = _rmsnorm(x2, gple_ref[...]).astype(jnp.bfloat16)
        gate = _sigmoid(jnp.dot(hn2, _weights(wpg_ref[...]), preferred_element_type=jnp.float32))
        pe = jnp.dot(ps.astype(jnp.bfloat16), _weights(wple_ref[...]), preferred_element_type=jnp.float32)
        x3 = x2 + gate * pe
        if final:
            x3 = _rmsnorm(x3, gfin_ref[...])
        return x3

    @pl.when(g == 0)
    def _():
        project(x_ref[0:SUB, :])
        activate(act0_ref)

    act_slots = (act0_ref, act1_ref)
    for k in range(GROUPS):
        rows = slice(k * SUB, (k + 1) * SUB)
        nxt = slice((k + 1) * SUB, (k + 2) * SUB)
        x2 = down(act_slots[k % 2], x_ref[rows, :])
        project(x_ref[nxt, :] if k + 1 < GROUPS else xn_ref[...])
        activate(act_slots[(k + 1) % 2])
        o_ref[rows, :] = embed(x2, p_ref[rows, :])


def _resident(shape):
    zeros = (0,) * len(shape)
    return pl.BlockSpec(shape, lambda g: zeros, pipeline_mode=pl.Buffered(1))


def _resident_layer(stacked_shape, layer):
    index = (layer,) + (0,) * (len(stacked_shape) - 1)
    return pl.BlockSpec((None,) + tuple(stacked_shape[1:]), lambda g: index,
                        pipeline_mode=pl.Buffered(1))


def _row_specs(n, d):
    last_group = n // SUB - 1
    cur = pl.BlockSpec((TILE, d), lambda g: (g, 0))
    nxt = pl.BlockSpec((SUB, d), lambda g: (jnp.minimum(GROUPS * (g + 1), last_group), 0))
    return cur, nxt


def _mixer_call(x2d, gmix, win, c4w, c4b, wr, br, wi, bi, lam, c3w, wout, *, layer, seq_len):
    n, d = x2d.shape
    w_in_cols = win.shape[-1]
    f32 = jnp.float32
    cur, nxt = _row_specs(n, d)
    consts = (gmix, win, c4w, c4b, wr, br, wi, bi, lam, c3w, wout)
    stacked = (win, wr, wi, wout)
    return pl.pallas_call(
        functools.partial(_mixer_kernel, groups_per_seq=seq_len // SUB, d=d),
        grid=(n // TILE,),
        in_specs=[cur, nxt] + [
            _resident_layer(c.shape, layer) if any(c is s for s in stacked)
            else _resident(c.shape) for c in consts],
        out_specs=cur,
        out_shape=jax.ShapeDtypeStruct((n, d), f32),
        scratch_shapes=[
            pltpu.VMEM((SUB, w_in_cols), f32),
            pltpu.VMEM((SUB, w_in_cols), f32),
            pltpu.VMEM((SUB, d), f32),
            pltpu.VMEM((SUB + _H3 * SUBLANES, d), f32),
            pltpu.VMEM((SUB + _H3 * SUBLANES, d), f32),
            pltpu.VMEM((2, SUB, d), f32),
            pltpu.VMEM((2, SUB, d), f32),
            pltpu.VMEM((SUB, d), f32),
            pltpu.VMEM((SUB, d), f32),
            pltpu.VMEM((SUB, d), f32),
            pltpu.VMEM((SUB, d), f32),
            pltpu.VMEM((SUB, d), f32),
            pltpu.VMEM((_H4 * SUBLANES, d), f32),
            pltpu.VMEM((_H3 * SUBLANES, d), f32),
            pltpu.VMEM((SUBLANES, d), f32),
            pltpu.VMEM((SUBLANES, d), f32),
        ],
        compiler_params=pltpu.CompilerParams(
            dimension_semantics=("arbitrary",), vmem_limit_bytes=VMEM_LIMIT_BYTES),
        name="mixer",
    )(x2d, x2d, *consts)


def _ffn_call(x2d, p2d, gffn, wgu, wd, gple, wpg, wple, gfin, *, layer, final):
    n, d = x2d.shape
    d_ff = wgu.shape[-1] // 2
    cur, nxt = _row_specs(n, d)
    p_spec = pl.BlockSpec((TILE, p2d.shape[1]), lambda g: (g, 0))
    consts = (gffn, wgu, wd, gple, wpg, wple, gfin)
    stacked = (wgu, wd, wpg, wple)
    return pl.pallas_call(
        functools.partial(_ffn_kernel, d_ff=d_ff, final=final),
        grid=(n // TILE,),
        in_specs=[cur, nxt, p_spec] + [
            _resident_layer(c.shape, layer) if any(c is s for s in stacked)
            else _resident(c.shape) for c in consts],
        out_specs=cur,
        out_shape=jax.ShapeDtypeStruct((n, d), jnp.float32),
        scratch_shapes=[pltpu.VMEM((SUB, 2 * d_ff), jnp.float32),
                        pltpu.VMEM((SUB, d_ff), jnp.bfloat16),
                        pltpu.VMEM((SUB, d_ff), jnp.bfloat16)],
        compiler_params=pltpu.CompilerParams(
            dimension_semantics=("arbitrary",), vmem_limit_bytes=VMEM_LIMIT_BYTES),
        name="ffn_ple",
    )(x2d, x2d, p2d, *consts)


def kernel(x, p, g_mix, w_in, conv4_w, conv4_b, w_rg_r, b_rg_r, w_rg_i, b_rg_i, lru_lambda, conv3_w, w_out, g_ffn, w_gate_up, w_down, g_ple, w_ple_gate, w_ple, g_final):
    bsz, seq_len, d = x.shape
    depth = p.shape[0]
    assert seq_len % TILE == 0 and GROUPS % 2 == 0 and d % (RG_HEADS * LANES) == 0
    row = lambda v: v.reshape(1, -1)
    rows8 = lambda v: jnp.repeat(v.reshape(-1, v.shape[-1]), SUBLANES, axis=0)

    def pack_heads(w):
        hb = w.shape[-1]
        return _pack_weights(w.reshape(-1, hb, hb)).reshape(depth, -1, hb // 2, hb)

    win_p, wout_p = _pack_weights(w_in), _pack_weights(w_out)
    wr_p, wi_p = pack_heads(w_rg_r), pack_heads(w_rg_i)
    wgu_p, wd_p = _pack_weights(w_gate_up), _pack_weights(w_down)
    wpg_p, wple_p = _pack_weights(w_ple_gate), _pack_weights(w_ple)

    xs = _permute_rows(x.reshape(bsz * seq_len, d))
    for i in range(depth):
        xs = _mixer_call(
            xs, row(g_mix[i]), win_p, rows8(conv4_w[i]), rows8(conv4_b[i]),
            wr_p, rows8(b_rg_r[i].reshape(-1)), wi_p,
            rows8(b_rg_i[i].reshape(-1)), rows8(lru_lambda[i]), rows8(conv3_w[i]),
            wout_p, layer=i, seq_len=seq_len)
        p_i = _permute_rows(p[i].reshape(bsz * seq_len, -1))
        xs = _ffn_call(
            xs, p_i, row(g_ffn[i]), wgu_p, wd_p,
            row(g_ple[i]), wpg_p, wple_p,
            row(g_final), layer=i, final=(i == depth - 1))
    return _unpermute_rows(xs).reshape(bsz, seq_len, d)
```

```python
import functools

import jax
import jax.numpy as jnp
from jax import lax
from jax.experimental import pallas as pl
from jax.experimental.pallas import tpu as pltpu

EPS = 1e-6
RG_C = 8.0
RG_HEADS = 4
CONV4_WIDTH = 4
CONV3_WIDTH = 3

SUBLANES = 8
LANES = 128
PACK_BLOCK_ROWS = 512
PACK_BLOCK_COLS = 2048
SUB = 128
J = SUB // SUBLANES
MIXER_GROUPS = 2
FFN_GROUPS = 4
VMEM_LIMIT_BYTES = 60000 * 1024

_H3 = CONV3_WIDTH - 1
_H4 = CONV4_WIDTH - 1


def _permute_rows(a):
    n, c = a.shape
    return a.reshape(n // SUB, SUBLANES, J, c).transpose(0, 2, 1, 3).reshape(n, c)


def _unpermute_rows(a):
    n, c = a.shape
    return a.reshape(n // SUB, J, SUBLANES, c).transpose(0, 2, 1, 3).reshape(n, c)


def _rmsnorm(xs, g):
    ms = jnp.mean(xs * xs, axis=-1, keepdims=True)
    return (xs * lax.rsqrt(ms + EPS)) * g


def _sigmoid(z):
    return 0.5 * jnp.tanh(0.5 * z) + 0.5


def _softplus(z):
    return jnp.maximum(z, 0.0) + jnp.log1p(jnp.exp(-jnp.abs(z)))


def _rows(j, n=1):
    return slice(j * SUBLANES, (j + n) * SUBLANES)


def _largest_block(size, unit, cap):
    return max(b for b in range(unit, min(size, cap) + 1, unit) if size % b == 0)


def _pack_kernel(w_ref, o_ref):
    o_ref[...] = pltpu.bitcast(w_ref[...].astype(jnp.bfloat16), jnp.uint32)


def _pack_weights(w):
    layers, k, n = w.shape
    bk = _largest_block(k, 2 * SUBLANES, PACK_BLOCK_ROWS)
    bn = _largest_block(n, LANES, PACK_BLOCK_COLS)
    return pl.pallas_call(
        _pack_kernel,
        grid=(layers, k // bk, n // bn),
        in_specs=[pl.BlockSpec((None, bk, bn), lambda l, i, j: (l, i, j))],
        out_specs=pl.BlockSpec((None, bk // 2, bn), lambda l, i, j: (l, i, j)),
        out_shape=jax.ShapeDtypeStruct((layers, k // 2, n), jnp.uint32),
        name="pack_weights",
    )(w)


def _weights(packed):
    return pltpu.bitcast(packed, jnp.bfloat16)


def _mixer_kernel(x_ref, xn_ref, gmix_ref, win_ref, c4w_ref, c4b_ref, wr_ref, br_ref,
                  wi_ref, bi_ref, lam_ref, c3w_ref, wout_ref, o_ref,
                  hn0_ref, hn1_ref, z0_ref, z1_ref, xc0_ref, xc1_ref, q0_ref, q1_ref,
                  rp_ref, ip_ref, a_ref, u_ref, hl_ref, pl_ref, mrg_ref,
                  p4_ref, pq_ref, hc_ref, c_ref, *, groups, groups_per_seq, d):
    g = pl.program_id(0)
    hb = d // RG_HEADS
    o_rx, o_ry, o_cb, o_cc, o_cx, o_gr, o_gc = (k * d for k in range(7))
    first_stream = lax.broadcasted_iota(jnp.int32, (SUBLANES, hb), 0) == 0
    heads = [slice(h * hb, (h + 1) * hb) for h in range(RG_HEADS)]
    chunks = [(o_rx, o_ry), (o_cc, o_gr), (o_ry, o_cc), (o_gr, o_gc + d)]

    def zcols(off, cols):
        return slice(off + cols.start, off + cols.stop)

    def from_prev_stream(cur, prev):
        return jnp.where(first_stream, pltpu.roll(prev, 1, 0), pltpu.roll(cur, 1, 0))

    def normalize(xs, hn_ref):
        hn_ref[...] = _rmsnorm(xs, gmix_ref[...]).astype(jnp.bfloat16)

    def project(hn_ref, z_ref, chunk):
        lo, hi = chunks[chunk]
        z_ref[:, lo:hi] = jnp.dot(hn_ref[...], _weights(win_ref[:, lo:hi]),
                                  preferred_element_type=jnp.float32)

    def convs(z_ref, xc_ref, q_ref, cols, seq_start):
        def history(h_ref, jj):
            return jnp.where(seq_start, 0.0, h_ref[_rows(jj), cols])

        c_rx = zcols(o_rx, cols)
        halo4 = [from_prev_stream(z_ref[_rows(J - _H4 + jj), c_rx], history(p4_ref, jj))
                 for jj in range(_H4)]
        p4_ref[:, cols] = z_ref[_rows(J - _H4, _H4), c_rx]

        def rx(j):
            return halo4[j + _H4] if j < 0 else z_ref[_rows(j), c_rx]

        for j in range(J):
            acc = c4b_ref[:, cols] + c4w_ref[_rows(0), cols] * rx(j - _H4)
            for k in range(1, CONV4_WIDTH):
                acc = acc + c4w_ref[_rows(k), cols] * rx(j - _H4 + k)
            xc_ref[_rows(j), cols] = acc

        for j in range(J):
            q_ref[_rows(_H3 + j), cols] = (z_ref[_rows(j), zcols(o_cc, cols)]
                                           * z_ref[_rows(j), zcols(o_cx, cols)])
        for jj in range(_H3):
            q_ref[_rows(jj), cols] = from_prev_stream(q_ref[_rows(J + jj), cols],
                                                      history(pq_ref, jj))
        pq_ref[:, cols] = q_ref[_rows(J, _H3), cols]

    def gates(xc_ref, h, cols):
        xh = xc_ref[:, cols]
        xb = xh.astype(jnp.bfloat16)
        slot = lax.rem(g, 2)
        rp_ref[slot, :, cols] = jnp.dot(xb, _weights(wr_ref[h]), preferred_element_type=jnp.float32)
        ip_ref[slot, :, cols] = jnp.dot(xb, _weights(wi_ref[h]), preferred_element_type=jnp.float32)
        log_a_scale = -RG_C * _softplus(-lam_ref[:, cols])
        for j in range(J):
            r = _sigmoid(rp_ref[slot, _rows(j), cols] + br_ref[:, cols])
            gi = _sigmoid(ip_ref[slot, _rows(j), cols] + bi_ref[:, cols])
            log_a = log_a_scale * r
            a = jnp.exp(log_a)
            mult = jnp.sqrt(-jnp.tanh(log_a) * (1.0 + a * a))
            a_ref[_rows(j), cols] = a
            u_ref[_rows(j), cols] = mult * (gi * xh[_rows(j), :])

    def scan_merge(z_ref, q_ref, cols, seq_start):
        hloc = jnp.zeros((SUBLANES, hb), jnp.float32)
        prod = jnp.ones((SUBLANES, hb), jnp.float32)
        for j in range(J):
            a = a_ref[_rows(j), cols]
            hloc = a * hloc + u_ref[_rows(j), cols]
            prod = a * prod
            hl_ref[_rows(j), cols] = hloc
            pl_ref[_rows(j), cols] = prod

        cur = jnp.where(seq_start, 0.0, hc_ref[0:1, cols])
        last = (J - 1) * SUBLANES
        for s in range(SUBLANES):
            c_ref[s:s + 1, cols] = cur
            cur = hl_ref[last + s:last + s + 1, cols] + pl_ref[last + s:last + s + 1, cols] * cur
        hc_ref[0:1, cols] = cur
        cvec = c_ref[:, cols]

        for j in range(J):
            hfull = hl_ref[_rows(j), cols] + pl_ref[_rows(j), cols] * cvec
            y_rnn = jax.nn.gelu(z_ref[_rows(j), zcols(o_ry, cols)]) * hfull
            yc = c3w_ref[_rows(0), cols] * q_ref[_rows(j), cols]
            for k in range(1, CONV3_WIDTH):
                yc = yc + c3w_ref[_rows(k), cols] * q_ref[_rows(j + k), cols]
            y_conv = z_ref[_rows(j), zcols(o_cb, cols)] * yc
            mrg_ref[_rows(j), cols] = (
                _sigmoid(z_ref[_rows(j), zcols(o_gr, cols)]) * y_rnn
                + _sigmoid(z_ref[_rows(j), zcols(o_gc, cols)]) * y_conv)

    def output(xs):
        return xs + jnp.dot(mrg_ref[...].astype(jnp.bfloat16), _weights(wout_ref[...]),
                            preferred_element_type=jnp.float32)

    starts = [(groups * g + k) % groups_per_seq == 0 for k in range(groups + 1)]
    slots = ((hn0_ref, z0_ref, xc0_ref, q0_ref), (hn1_ref, z1_ref, xc1_ref, q1_ref))

    @pl.when(g == 0)
    def _():
        p4_ref[...] = jnp.zeros_like(p4_ref)
        pq_ref[...] = jnp.zeros_like(pq_ref)
        hc_ref[...] = jnp.zeros_like(hc_ref)
        normalize(x_ref[0:SUB, :], hn0_ref)
        for chunk in range(len(chunks)):
            project(hn0_ref, z0_ref, chunk)
        for cols in heads:
            convs(z0_ref, xc0_ref, q0_ref, cols, True)

    for k in range(groups):
        rows = slice(k * SUB, (k + 1) * SUB)
        nxt = slice((k + 1) * SUB, (k + 2) * SUB)
        _, z_cur, xc_cur, q_cur = slots[k % 2]
        hn_nxt, z_nxt, xc_nxt, q_nxt = slots[(k + 1) % 2]
        normalize(x_ref[nxt, :] if k + 1 < groups else xn_ref[...], hn_nxt)
        for h, cols in enumerate(heads):
            gates(xc_cur, h, cols)
            project(hn_nxt, z_nxt, h)
            scan_merge(z_cur, q_cur, cols, starts[k])
        o_ref[rows, :] = output(x_ref[rows, :])
        for cols in heads:
            convs(z_nxt, xc_nxt, q_nxt, cols, starts[k + 1])


def _ffn_kernel(x_ref, xn_ref, p_ref, gffn_ref, wgu_ref, wd_ref, gple_ref, wpg_ref,
                wple_ref, gfin_ref, o_ref, gu_ref, act0_ref, act1_ref, *, groups, d_ff, final):
    g = pl.program_id(0)

    def project(xs):
        hn = _rmsnorm(xs, gffn_ref[...]).astype(jnp.bfloat16)
        gu_ref[...] = jnp.dot(hn, _weights(wgu_ref[...]), preferred_element_type=jnp.float32)

    def activate(act_ref):
        gate = gu_ref[:, 0:d_ff]
        act_ref[...] = ((gate * _sigmoid(gate)) * gu_ref[:, d_ff:2 * d_ff]).astype(jnp.bfloat16)

    def down(act_ref, xs):
        return xs + jnp.dot(act_ref[...], _weights(wd_ref[...]), preferred_element_type=jnp.float32)

    def embed(x2, ps):
        hn2 = _rmsnorm(x2, gple_ref[...]).astype(jnp.bfloat16)
        gate = _sigmoid(jnp.dot(hn2, _weights(wpg_ref[...]), preferred_element_type=jnp.float32))
        pe = jnp.dot(ps.astype(jnp.bfloat16), _weights(wple_ref[...]), preferred_element_type=jnp.float32)
        x3 = x2 + gate * pe
        if final:
            x3 = _rmsnorm(x3, gfin_ref[...])
        return x3

    @pl.when(g == 0)
    def _():
        project(x_ref[0:SUB, :])
        activate(act0_ref)

    act_slots = (act0_ref, act1_ref)
    for k in range(groups):
        rows = slice(k * SUB, (k + 1) * SUB)
        nxt = slice((k + 1) * SUB, (k + 2) * SUB)
        x2 = down(act_slots[k % 2], x_ref[rows, :])
        project(x_ref[nxt, :] if k + 1 < groups else xn_ref[...])
        activate(act_slots[(k + 1) % 2])
        o_ref[rows, :] = embed(x2, p_ref[rows, :])


def _resident(shape):
    zeros = (0,) * len(shape)
    return pl.BlockSpec(shape, lambda g: zeros, pipeline_mode=pl.Buffered(1))


def _resident_layer(stacked_shape, layer):
    index = (layer,) + (0,) * (len(stacked_shape) - 1)
    return pl.BlockSpec((None,) + tuple(stacked_shape[1:]), lambda g: index,
                        pipeline_mode=pl.Buffered(1))


def _row_specs(n, d, groups):
    last_group = n // SUB - 1
    cur = pl.BlockSpec((groups * SUB, d), lambda g: (g, 0))
    nxt = pl.BlockSpec((SUB, d), lambda g: (jnp.minimum(groups * (g + 1), last_group), 0))
    return cur, nxt


def _mixer_call(x2d, gmix, win, c4w, c4b, wr, br, wi, bi, lam, c3w, wout, *, layer, seq_len):
    n, d = x2d.shape
    w_in_cols = win.shape[-1]
    f32 = jnp.float32
    groups = MIXER_GROUPS
    cur, nxt = _row_specs(n, d, groups)
    consts = (gmix, win, c4w, c4b, wr, br, wi, bi, lam, c3w, wout)
    stacked = (win, wr, wi, wout)
    return pl.pallas_call(
        functools.partial(_mixer_kernel, groups=groups, groups_per_seq=seq_len // SUB, d=d),
        grid=(n // (groups * SUB),),
        in_specs=[cur, nxt] + [
            _resident_layer(c.shape, layer) if any(c is s for s in stacked)
            else _resident(c.shape) for c in consts],
        out_specs=cur,
        out_shape=jax.ShapeDtypeStruct((n, d), f32),
        scratch_shapes=[
            pltpu.VMEM((SUB, d), jnp.bfloat16),
            pltpu.VMEM((SUB, d), jnp.bfloat16),
            pltpu.VMEM((SUB, w_in_cols), f32),
            pltpu.VMEM((SUB, w_in_cols), f32),
            pltpu.VMEM((SUB, d), f32),
            pltpu.VMEM((SUB, d), f32),
            pltpu.VMEM((SUB + _H3 * SUBLANES, d), f32),
            pltpu.VMEM((SUB + _H3 * SUBLANES, d), f32),
            pltpu.VMEM((2, SUB, d), f32),
            pltpu.VMEM((2, SUB, d), f32),
            pltpu.VMEM((SUB, d), f32),
            pltpu.VMEM((SUB, d), f32),
            pltpu.VMEM((SUB, d), f32),
            pltpu.VMEM((SUB, d), f32),
            pltpu.VMEM((SUB, d), f32),
            pltpu.VMEM((_H4 * SUBLANES, d), f32),
            pltpu.VMEM((_H3 * SUBLANES, d), f32),
            pltpu.VMEM((SUBLANES, d), f32),
            pltpu.VMEM((SUBLANES, d), f32),
        ],
        compiler_params=pltpu.CompilerParams(
            dimension_semantics=("arbitrary",), vmem_limit_bytes=VMEM_LIMIT_BYTES),
        name="mixer",
    )(x2d, x2d, *consts)


def _ffn_call(x2d, p2d, gffn, wgu, wd, gple, wpg, wple, gfin, *, layer, final):
    n, d = x2d.shape
    d_ff = wgu.shape[-1] // 2
    groups = FFN_GROUPS
    cur, nxt = _row_specs(n, d, groups)
    p_spec = pl.BlockSpec((groups * SUB, p2d.shape[1]), lambda g: (g, 0))
    consts = (gffn, wgu, wd, gple, wpg, wple, gfin)
    stacked = (wgu, wd, wpg, wple)
    return pl.pallas_call(
        functools.partial(_ffn_kernel, groups=groups, d_ff=d_ff, final=final),
        grid=(n // (groups * SUB),),
        in_specs=[cur, nxt, p_spec] + [
            _resident_layer(c.shape, layer) if any(c is s for s in stacked)
            else _resident(c.shape) for c in consts],
        out_specs=cur,
        out_shape=jax.ShapeDtypeStruct((n, d), jnp.float32),
        scratch_shapes=[pltpu.VMEM((SUB, 2 * d_ff), jnp.float32),
                        pltpu.VMEM((SUB, d_ff), jnp.bfloat16),
                        pltpu.VMEM((SUB, d_ff), jnp.bfloat16)],
        compiler_params=pltpu.CompilerParams(
            dimension_semantics=("arbitrary",), vmem_limit_bytes=VMEM_LIMIT_BYTES),
        name="ffn_ple",
    )(x2d, x2d, p2d, *consts)


def kernel(x, p, g_mix, w_in, conv4_w, conv4_b, w_rg_r, b_rg_r, w_rg_i, b_rg_i, lru_lambda, conv3_w, w_out, g_ffn, w_gate_up, w_down, g_ple, w_ple_gate, w_ple, g_final):
    bsz, seq_len, d = x.shape
    depth = p.shape[0]
    for groups in (MIXER_GROUPS, FFN_GROUPS):
        assert groups % 2 == 0 and seq_len % (groups * SUB) == 0
    assert d % (RG_HEADS * LANES) == 0
    row = lambda v: v.reshape(1, -1)
    rows8 = lambda v: jnp.repeat(v.reshape(-1, v.shape[-1]), SUBLANES, axis=0)

    def pack_heads(w):
        hb = w.shape[-1]
        return _pack_weights(w.reshape(-1, hb, hb)).reshape(depth, -1, hb // 2, hb)

    win_p, wout_p = _pack_weights(w_in), _pack_weights(w_out)
    wr_p, wi_p = pack_heads(w_rg_r), pack_heads(w_rg_i)
    wgu_p, wd_p = _pack_weights(w_gate_up), _pack_weights(w_down)
    wpg_p, wple_p = _pack_weights(w_ple_gate), _pack_weights(w_ple)

    xs = _permute_rows(x.reshape(bsz * seq_len, d))
    for i in range(depth):
        xs = _mixer_call(
            xs, row(g_mix[i]), win_p, rows8(conv4_w[i]), rows8(conv4_b[i]),
            wr_p, rows8(b_rg_r[i].reshape(-1)), wi_p,
            rows8(b_rg_i[i].reshape(-1)), rows8(lru_lambda[i]), rows8(conv3_w[i]),
            wout_p, layer=i, seq_len=seq_len)
        p_i = _permute_rows(p[i].reshape(bsz * seq_len, -1))
        xs = _ffn_call(
            xs, p_i, row(g_ffn[i]), wgu_p, wd_p,
            row(g_ple[i]), wpg_p, wple_p,
            row(g_final), layer=i, final=(i == depth - 1))
    return _unpermute_rows(xs).reshape(bsz, seq_len, d)
```

```python
import functools

import jax
import jax.numpy as jnp
from jax import lax
from jax.experimental import pallas as pl
from jax.experimental.pallas import tpu as pltpu

EPS = 1e-6
RG_C = 8.0
RG_HEADS = 4
CONV4_WIDTH = 4
CONV3_WIDTH = 3

SUBLANES = 8
LANES = 128
PACK_BLOCK_ROWS = 512
PACK_BLOCK_COLS = 2048
SUB = 128
J = SUB // SUBLANES
MIXER_GROUPS = 2
FFN_GROUPS = 4
FFN_SUB = 128
VMEM_LIMIT_BYTES = 60000 * 1024

_H3 = CONV3_WIDTH - 1
_H4 = CONV4_WIDTH - 1


def _permute_rows(a):
    n, c = a.shape
    return a.reshape(n // SUB, SUBLANES, J, c).transpose(0, 2, 1, 3).reshape(n, c)


def _unpermute_rows(a):
    n, c = a.shape
    return a.reshape(n // SUB, J, SUBLANES, c).transpose(0, 2, 1, 3).reshape(n, c)


def _rmsnorm(xs, g):
    ms = jnp.mean(xs * xs, axis=-1, keepdims=True)
    return (xs * lax.rsqrt(ms + EPS)) * g


def _sigmoid(z):
    return 0.5 * jnp.tanh(0.5 * z) + 0.5


def _softplus(z):
    return jnp.maximum(z, 0.0) + jnp.log1p(jnp.exp(-jnp.abs(z)))


def _rows(j, n=1):
    return slice(j * SUBLANES, (j + n) * SUBLANES)


def _largest_block(size, unit, cap):
    return max(b for b in range(unit, min(size, cap) + 1, unit) if size % b == 0)


def _pack_kernel(w_ref, o_ref):
    o_ref[...] = pltpu.bitcast(w_ref[...].astype(jnp.bfloat16), jnp.uint32)


def _pack_weights(w):
    layers, k, n = w.shape
    bk = _largest_block(k, 2 * SUBLANES, PACK_BLOCK_ROWS)
    bn = _largest_block(n, LANES, PACK_BLOCK_COLS)
    return pl.pallas_call(
        _pack_kernel,
        grid=(layers, k // bk, n // bn),
        in_specs=[pl.BlockSpec((None, bk, bn), lambda l, i, j: (l, i, j))],
        out_specs=pl.BlockSpec((None, bk // 2, bn), lambda l, i, j: (l, i, j)),
        out_shape=jax.ShapeDtypeStruct((layers, k // 2, n), jnp.uint32),
        name="pack_weights",
    )(w)


def _weights(packed):
    return pltpu.bitcast(packed, jnp.bfloat16)


def _mixer_kernel(x_ref, xn_ref, gmix_ref, win_ref, c4w_ref, c4b_ref, wr_ref, br_ref,
                  wi_ref, bi_ref, lam_ref, c3w_ref, wout_ref, o_ref,
                  hn0_ref, hn1_ref, z0_ref, z1_ref, xc0_ref, xc1_ref, q0_ref, q1_ref,
                  rp_ref, ip_ref, a_ref, u_ref, hl_ref, pl_ref, mrg_ref,
                  p4_ref, pq_ref, hc_ref, c_ref, *, groups, groups_per_seq, d):
    g = pl.program_id(0)
    hb = d // RG_HEADS
    o_rx, o_ry, o_cb, o_cc, o_cx, o_gr, o_gc = (k * d for k in range(7))
    first_stream = lax.broadcasted_iota(jnp.int32, (SUBLANES, hb), 0) == 0
    heads = [slice(h * hb, (h + 1) * hb) for h in range(RG_HEADS)]
    chunks = [(o_rx, o_ry), (o_cc, o_gr), (o_ry, o_cc), (o_gr, o_gc + d)]

    def zcols(off, cols):
        return slice(off + cols.start, off + cols.stop)

    def from_prev_stream(cur, prev):
        return jnp.where(first_stream, pltpu.roll(prev, 1, 0), pltpu.roll(cur, 1, 0))

    def normalize(xs, hn_ref):
        hn_ref[...] = _rmsnorm(xs, gmix_ref[...]).astype(jnp.bfloat16)

    def project(hn_ref, z_ref, chunk):
        lo, hi = chunks[chunk]
        z_ref[:, lo:hi] = jnp.dot(hn_ref[...], _weights(win_ref[:, lo:hi]),
                                  preferred_element_type=jnp.float32)

    def convs(z_ref, xc_ref, q_ref, cols, seq_start):
        def history(h_ref, jj):
            return jnp.where(seq_start, 0.0, h_ref[_rows(jj), cols])

        c_rx = zcols(o_rx, cols)
        halo4 = [from_prev_stream(z_ref[_rows(J - _H4 + jj), c_rx], history(p4_ref, jj))
                 for jj in range(_H4)]
        p4_ref[:, cols] = z_ref[_rows(J - _H4, _H4), c_rx]

        def rx(j):
            return halo4[j + _H4] if j < 0 else z_ref[_rows(j), c_rx]

        for j in range(J):
            acc = c4b_ref[:, cols] + c4w_ref[_rows(0), cols] * rx(j - _H4)
            for k in range(1, CONV4_WIDTH):
                acc = acc + c4w_ref[_rows(k), cols] * rx(j - _H4 + k)
            xc_ref[_rows(j), cols] = acc

        for j in range(J):
            q_ref[_rows(_H3 + j), cols] = (z_ref[_rows(j), zcols(o_cc, cols)]
                                           * z_ref[_rows(j), zcols(o_cx, cols)])
        for jj in range(_H3):
            q_ref[_rows(jj), cols] = from_prev_stream(q_ref[_rows(J + jj), cols],
                                                      history(pq_ref, jj))
        pq_ref[:, cols] = q_ref[_rows(J, _H3), cols]

    def gates(xc_ref, h, cols):
        xh = xc_ref[:, cols]
        xb = xh.astype(jnp.bfloat16)
        slot = lax.rem(g, 2)
        rp_ref[slot, :, cols] = jnp.dot(xb, _weights(wr_ref[h]), preferred_element_type=jnp.float32)
        ip_ref[slot, :, cols] = jnp.dot(xb, _weights(wi_ref[h]), preferred_element_type=jnp.float32)
        log_a_scale = -RG_C * _softplus(-lam_ref[:, cols])
        for j in range(J):
            r = _sigmoid(rp_ref[slot, _rows(j), cols] + br_ref[:, cols])
            gi = _sigmoid(ip_ref[slot, _rows(j), cols] + bi_ref[:, cols])
            log_a = log_a_scale * r
            a = jnp.exp(log_a)
            mult = jnp.sqrt(-jnp.tanh(log_a) * (1.0 + a * a))
            a_ref[_rows(j), cols] = a
            u_ref[_rows(j), cols] = mult * (gi * xh[_rows(j), :])

    def scan_merge(z_ref, q_ref, cols, seq_start):
        hloc = jnp.zeros((SUBLANES, hb), jnp.float32)
        prod = jnp.ones((SUBLANES, hb), jnp.float32)
        for j in range(J):
            a = a_ref[_rows(j), cols]
            hloc = a * hloc + u_ref[_rows(j), cols]
            prod = a * prod
            hl_ref[_rows(j), cols] = hloc
            pl_ref[_rows(j), cols] = prod

        cur = jnp.where(seq_start, 0.0, hc_ref[0:1, cols])
        last = (J - 1) * SUBLANES
        for s in range(SUBLANES):
            c_ref[s:s + 1, cols] = cur
            cur = hl_ref[last + s:last + s + 1, cols] + pl_ref[last + s:last + s + 1, cols] * cur
        hc_ref[0:1, cols] = cur
        cvec = c_ref[:, cols]

        for j in range(J):
            hfull = hl_ref[_rows(j), cols] + pl_ref[_rows(j), cols] * cvec
            y_rnn = jax.nn.gelu(z_ref[_rows(j), zcols(o_ry, cols)]) * hfull
            yc = c3w_ref[_rows(0), cols] * q_ref[_rows(j), cols]
            for k in range(1, CONV3_WIDTH):
                yc = yc + c3w_ref[_rows(k), cols] * q_ref[_rows(j + k), cols]
            y_conv = z_ref[_rows(j), zcols(o_cb, cols)] * yc
            mrg_ref[_rows(j), cols] = (
                _sigmoid(z_ref[_rows(j), zcols(o_gr, cols)]) * y_rnn
                + _sigmoid(z_ref[_rows(j), zcols(o_gc, cols)]) * y_conv)

    def output(h, cols):
        w_rows = slice(cols.start // 2, cols.stop // 2)
        return jnp.dot(mrg_ref[:, cols].astype(jnp.bfloat16), _weights(wout_ref[w_rows, :]),
                       preferred_element_type=jnp.float32)

    starts = [(groups * g + k) % groups_per_seq == 0 for k in range(groups + 1)]
    slots = ((hn0_ref, z0_ref, xc0_ref, q0_ref), (hn1_ref, z1_ref, xc1_ref, q1_ref))

    @pl.when(g == 0)
    def _():
        p4_ref[...] = jnp.zeros_like(p4_ref)
        pq_ref[...] = jnp.zeros_like(pq_ref)
        hc_ref[...] = jnp.zeros_like(hc_ref)
        normalize(x_ref[0:SUB, :], hn0_ref)
        for chunk in range(len(chunks)):
            project(hn0_ref, z0_ref, chunk)
        for cols in heads:
            convs(z0_ref, xc0_ref, q0_ref, cols, True)

    for k in range(groups):
        rows = slice(k * SUB, (k + 1) * SUB)
        nxt = slice((k + 1) * SUB, (k + 2) * SUB)
        _, z_cur, xc_cur, q_cur = slots[k % 2]
        hn_nxt, z_nxt, xc_nxt, q_nxt = slots[(k + 1) % 2]
        normalize(x_ref[nxt, :] if k + 1 < groups else xn_ref[...], hn_nxt)
        acc = x_ref[rows, :]
        for h, cols in enumerate(heads):
            gates(xc_cur, h, cols)
            project(hn_nxt, z_nxt, h)
            scan_merge(z_cur, q_cur, cols, starts[k])
            if h > 0:
                acc = acc + output(h - 1, heads[h - 1])
        o_ref[rows, :] = acc + output(RG_HEADS - 1, heads[-1])
        for cols in heads:
            convs(z_nxt, xc_nxt, q_nxt, cols, starts[k + 1])


def _ffn_kernel(x_ref, xn_ref, p_ref, gffn_ref, wgu_ref, wd_ref, gple_ref, wpg_ref,
                wple_ref, gfin_ref, o_ref, gu_ref, act0_ref, act1_ref, *, groups, d_ff, final):
    g = pl.program_id(0)

    def project(xs):
        hn = _rmsnorm(xs, gffn_ref[...]).astype(jnp.bfloat16)
        gu_ref[...] = jnp.dot(hn, _weights(wgu_ref[...]), preferred_element_type=jnp.float32)

    def activate(act_ref):
        gate = gu_ref[:, 0:d_ff]
        act_ref[...] = ((gate * _sigmoid(gate)) * gu_ref[:, d_ff:2 * d_ff]).astype(jnp.bfloat16)

    def down(act_ref, xs):
        return xs + jnp.dot(act_ref[...], _weights(wd_ref[...]), preferred_element_type=jnp.float32)

    def embed(x2, ps):
        hn2 = _rmsnorm(x2, gple_ref[...]).astype(jnp.bfloat16)
        gate = _sigmoid(jnp.dot(hn2, _weights(wpg_ref[...]), preferred_element_type=jnp.float32))
        pe = jnp.dot(ps.astype(jnp.bfloat16), _weights(wple_ref[...]), preferred_element_type=jnp.float32)
        x3 = x2 + gate * pe
        if final:
            x3 = _rmsnorm(x3, gfin_ref[...])
        return x3

    @pl.when(g == 0)
    def _():
        project(x_ref[0:FFN_SUB, :])
        activate(act0_ref)

    act_slots = (act0_ref, act1_ref)
    for k in range(groups):
        rows = slice(k * FFN_SUB, (k + 1) * FFN_SUB)
        nxt = slice((k + 1) * FFN_SUB, (k + 2) * FFN_SUB)
        x2 = down(act_slots[k % 2], x_ref[rows, :])
        project(x_ref[nxt, :] if k + 1 < groups else xn_ref[...])
        activate(act_slots[(k + 1) % 2])
        o_ref[rows, :] = embed(x2, p_ref[rows, :])


def _resident(shape):
    zeros = (0,) * len(shape)
    return pl.BlockSpec(shape, lambda g: zeros, pipeline_mode=pl.Buffered(1))


def _resident_layer(stacked_shape, layer):
    index = (layer,) + (0,) * (len(stacked_shape) - 1)
    return pl.BlockSpec((None,) + tuple(stacked_shape[1:]), lambda g: index,
                        pipeline_mode=pl.Buffered(1))


def _row_specs(n, d, sub, groups):
    last_group = n // sub - 1
    cur = pl.BlockSpec((groups * sub, d), lambda g: (g, 0))
    nxt = pl.BlockSpec((sub, d), lambda g: (jnp.minimum(groups * (g + 1), last_group), 0))
    return cur, nxt


def _mixer_call(x2d, gmix, win, c4w, c4b, wr, br, wi, bi, lam, c3w, wout, *, layer, seq_len):
    n, d = x2d.shape
    w_in_cols = win.shape[-1]
    f32 = jnp.float32
    groups = MIXER_GROUPS
    cur, nxt = _row_specs(n, d, SUB, groups)
    consts = (gmix, win, c4w, c4b, wr, br, wi, bi, lam, c3w, wout)
    stacked = (win, wr, wi, wout)
    return pl.pallas_call(
        functools.partial(_mixer_kernel, groups=groups, groups_per_seq=seq_len // SUB, d=d),
        grid=(n // (groups * SUB),),
        in_specs=[cur, nxt] + [
            _resident_layer(c.shape, layer) if any(c is s for s in stacked)
            else _resident(c.shape) for c in consts],
        out_specs=cur,
        out_shape=jax.ShapeDtypeStruct((n, d), f32),
        scratch_shapes=[
            pltpu.VMEM((SUB, d), jnp.bfloat16),
            pltpu.VMEM((SUB, d), jnp.bfloat16),
            pltpu.VMEM((SUB, w_in_cols), f32),
            pltpu.VMEM((SUB, w_in_cols), f32),
            pltpu.VMEM((SUB, d), f32),
            pltpu.VMEM((SUB, d), f32),
            pltpu.VMEM((SUB + _H3 * SUBLANES, d), f32),
            pltpu.VMEM((SUB + _H3 * SUBLANES, d), f32),
            pltpu.VMEM((2, SUB, d), f32),
            pltpu.VMEM((2, SUB, d), f32),
            pltpu.VMEM((SUB, d), f32),
            pltpu.VMEM((SUB, d), f32),
            pltpu.VMEM((SUB, d), f32),
            pltpu.VMEM((SUB, d), f32),
            pltpu.VMEM((SUB, d), f32),
            pltpu.VMEM((_H4 * SUBLANES, d), f32),
            pltpu.VMEM((_H3 * SUBLANES, d), f32),
            pltpu.VMEM((SUBLANES, d), f32),
            pltpu.VMEM((SUBLANES, d), f32),
        ],
        compiler_params=pltpu.CompilerParams(
            dimension_semantics=("arbitrary",), vmem_limit_bytes=VMEM_LIMIT_BYTES),
        name="mixer",
    )(x2d, x2d, *consts)


def _ffn_call(x2d, p2d, gffn, wgu, wd, gple, wpg, wple, gfin, *, layer, final):
    n, d = x2d.shape
    d_ff = wgu.shape[-1] // 2
    groups, sub = FFN_GROUPS, FFN_SUB
    cur, nxt = _row_specs(n, d, sub, groups)
    p_spec = pl.BlockSpec((groups * sub, p2d.shape[1]), lambda g: (g, 0))
    consts = (gffn, wgu, wd, gple, wpg, wple, gfin)
    stacked = (wgu, wd, wpg, wple)
    return pl.pallas_call(
        functools.partial(_ffn_kernel, groups=groups, d_ff=d_ff, final=final),
        grid=(n // (groups * sub),),
        in_specs=[cur, nxt, p_spec] + [
            _resident_layer(c.shape, layer) if any(c is s for s in stacked)
            else _resident(c.shape) for c in consts],
        out_specs=cur,
        out_shape=jax.ShapeDtypeStruct((n, d), jnp.float32),
        scratch_shapes=[pltpu.VMEM((sub, 2 * d_ff), jnp.float32),
                        pltpu.VMEM((sub, d_ff), jnp.bfloat16),
                        pltpu.VMEM((sub, d_ff), jnp.bfloat16)],
        compiler_params=pltpu.CompilerParams(
            dimension_semantics=("arbitrary",), vmem_limit_bytes=VMEM_LIMIT_BYTES),
        name="ffn_ple",
    )(x2d, x2d, p2d, *consts)


def kernel(x, p, g_mix, w_in, conv4_w, conv4_b, w_rg_r, b_rg_r, w_rg_i, b_rg_i, lru_lambda, conv3_w, w_out, g_ffn, w_gate_up, w_down, g_ple, w_ple_gate, w_ple, g_final):
    bsz, seq_len, d = x.shape
    depth = p.shape[0]
    for groups, sub in ((MIXER_GROUPS, SUB), (FFN_GROUPS, FFN_SUB)):
        assert groups % 2 == 0 and seq_len % (groups * sub) == 0
    assert d % (RG_HEADS * LANES) == 0
    row = lambda v: v.reshape(1, -1)
    rows8 = lambda v: jnp.repeat(v.reshape(-1, v.shape[-1]), SUBLANES, axis=0)

    def pack_heads(w):
        hb = w.shape[-1]
        return _pack_weights(w.reshape(-1, hb, hb)).reshape(depth, -1, hb // 2, hb)

    win_p, wout_p = _pack_weights(w_in), _pack_weights(w_out)
    wr_p, wi_p = pack_heads(w_rg_r), pack_heads(w_rg_i)
    wgu_p, wd_p = _pack_weights(w_gate_up), _pack_weights(w_down)
    wpg_p, wple_p = _pack_weights(w_ple_gate), _pack_weights(w_ple)

    xs = _permute_rows(x.reshape(bsz * seq_len, d))
    for i in range(depth):
        xs = _mixer_call(
            xs, row(g_mix[i]), win_p, rows8(conv4_w[i]), rows8(conv4_b[i]),
            wr_p, rows8(b_rg_r[i].reshape(-1)), wi_p,
            rows8(b_rg_i[i].reshape(-1)), rows8(lru_lambda[i]), rows8(conv3_w[i]),
            wout_p, layer=i, seq_len=seq_len)
        p_i = _permute_rows(p[i].reshape(bsz * seq_len, -1))
        xs = _ffn_call(
            xs, p_i, row(g_ffn[i]), wgu_p, wd_p,
            row(g_ple[i]), wpg_p, wple_p,
            row(g_final), layer=i, final=(i == depth - 1))
    return _unpermute_rows(xs).reshape(bsz, seq_len, d)
```

```python
import functools

import jax
import jax.numpy as jnp
import numpy as np
from jax import lax
from jax.experimental import pallas as pl
from jax.experimental.pallas import tpu as pltpu

EPS = 1e-6
RG_C = 8.0
RG_HEADS = 4
CONV4_WIDTH = 4
CONV3_WIDTH = 3

SUBLANES = 8
LANES = 128
PACK_BLOCK_ROWS = 512
PACK_BLOCK_COLS = 2048
SUB = 128
J = SUB // SUBLANES
MIXER_GROUPS = 2
FFN_GROUPS = 4
FFN_SUB = 128
VMEM_LIMIT_BYTES = 60000 * 1024

_H3 = CONV3_WIDTH - 1
_H4 = CONV4_WIDTH - 1


def _permute_rows(a):
    n, c = a.shape
    return a.reshape(n // SUB, SUBLANES, J, c).transpose(0, 2, 1, 3).reshape(n, c)


def _rmsnorm(xs, g):
    ms = jnp.mean(xs * xs, axis=-1, keepdims=True)
    return (xs * lax.rsqrt(ms + EPS)) * g


def _sigmoid(z):
    return 0.5 * jnp.tanh(0.5 * z) + 0.5


def _softplus(z):
    return jnp.maximum(z, 0.0) + jnp.log1p(jnp.exp(-jnp.abs(z)))


def _rows(j, n=1):
    return slice(j * SUBLANES, (j + n) * SUBLANES)


def _largest_block(size, unit, cap):
    return max(b for b in range(unit, min(size, cap) + 1, unit) if size % b == 0)


def _pack_kernel(w_ref, o_ref):
    o_ref[...] = pltpu.bitcast(w_ref[...].astype(jnp.bfloat16), jnp.uint32)


def _pack_weights(w):
    layers, k, n = w.shape
    bk = _largest_block(k, 2 * SUBLANES, PACK_BLOCK_ROWS)
    bn = _largest_block(n, LANES, PACK_BLOCK_COLS)
    return pl.pallas_call(
        _pack_kernel,
        grid=(layers, k // bk, n // bn),
        in_specs=[pl.BlockSpec((None, bk, bn), lambda l, i, j: (l, i, j))],
        out_specs=pl.BlockSpec((None, bk // 2, bn), lambda l, i, j: (l, i, j)),
        out_shape=jax.ShapeDtypeStruct((layers, k // 2, n), jnp.uint32),
        name="pack_weights",
    )(w)


def _weights(packed):
    return pltpu.bitcast(packed, jnp.bfloat16)


def _mixer_kernel(x_ref, xn_ref, gmix_ref, win_ref, c4w_ref, c4b_ref, wr_ref, br_ref,
                  wi_ref, bi_ref, lam_ref, c3w_ref, wout_ref, o_ref,
                  hn0_ref, hn1_ref, z0_ref, z1_ref, xc0_ref, xc1_ref, q0_ref, q1_ref,
                  rp_ref, ip_ref, a_ref, u_ref, hl_ref, pl_ref, mrg_ref,
                  p4_ref, pq_ref, hc_ref, c_ref, *, groups, groups_per_seq, d):
    g = pl.program_id(0)
    hb = d // RG_HEADS
    o_rx, o_ry, o_cb, o_cc, o_cx, o_gr, o_gc = (k * d for k in range(7))
    first_stream = lax.broadcasted_iota(jnp.int32, (SUBLANES, hb), 0) == 0
    heads = [slice(h * hb, (h + 1) * hb) for h in range(RG_HEADS)]
    chunks = [(o_rx, o_ry), (o_cc, o_gr), (o_ry, o_cc), (o_gr, o_gc + d)]

    def zcols(off, cols):
        return slice(off + cols.start, off + cols.stop)

    def from_prev_stream(cur, prev):
        return jnp.where(first_stream, pltpu.roll(prev, 1, 0), pltpu.roll(cur, 1, 0))

    def normalize(xs, hn_ref):
        hn_ref[...] = _rmsnorm(xs, gmix_ref[...]).astype(jnp.bfloat16)

    def project(hn_ref, z_ref, chunk):
        lo, hi = chunks[chunk]
        z_ref[:, lo:hi] = jnp.dot(hn_ref[...], _weights(win_ref[:, lo:hi]),
                                  preferred_element_type=jnp.float32)

    def convs(z_ref, xc_ref, q_ref, cols, seq_start):
        def history(h_ref, jj):
            return jnp.where(seq_start, 0.0, h_ref[_rows(jj), cols])

        c_rx = zcols(o_rx, cols)
        halo4 = [from_prev_stream(z_ref[_rows(J - _H4 + jj), c_rx], history(p4_ref, jj))
                 for jj in range(_H4)]
        p4_ref[:, cols] = z_ref[_rows(J - _H4, _H4), c_rx]

        def rx(j):
            return halo4[j + _H4] if j < 0 else z_ref[_rows(j), c_rx]

        for j in range(J):
            acc = c4b_ref[:, cols] + c4w_ref[_rows(0), cols] * rx(j - _H4)
            for k in range(1, CONV4_WIDTH):
                acc = acc + c4w_ref[_rows(k), cols] * rx(j - _H4 + k)
            xc_ref[_rows(j), cols] = acc

        for j in range(J):
            q_ref[_rows(_H3 + j), cols] = (z_ref[_rows(j), zcols(o_cc, cols)]
                                           * z_ref[_rows(j), zcols(o_cx, cols)])
        for jj in range(_H3):
            q_ref[_rows(jj), cols] = from_prev_stream(q_ref[_rows(J + jj), cols],
                                                      history(pq_ref, jj))
        pq_ref[:, cols] = q_ref[_rows(J, _H3), cols]

    def gates(xc_ref, h, cols):
        xh = xc_ref[:, cols]
        xb = xh.astype(jnp.bfloat16)
        slot = lax.rem(g, 2)
        rp_ref[slot, :, cols] = jnp.dot(xb, _weights(wr_ref[h]), preferred_element_type=jnp.float32)
        ip_ref[slot, :, cols] = jnp.dot(xb, _weights(wi_ref[h]), preferred_element_type=jnp.float32)
        log_a_scale = -RG_C * _softplus(-lam_ref[:, cols])
        for j in range(J):
            r = _sigmoid(rp_ref[slot, _rows(j), cols] + br_ref[:, cols])
            gi = _sigmoid(ip_ref[slot, _rows(j), cols] + bi_ref[:, cols])
            log_a = log_a_scale * r
            a = jnp.exp(log_a)
            mult = jnp.sqrt(-jnp.tanh(log_a) * (1.0 + a * a))
            a_ref[_rows(j), cols] = a
            u_ref[_rows(j), cols] = mult * (gi * xh[_rows(j), :])

    def scan_merge(z_ref, q_ref, cols, seq_start):
        hloc = jnp.zeros((SUBLANES, hb), jnp.float32)
        prod = jnp.ones((SUBLANES, hb), jnp.float32)
        for j in range(J):
            a = a_ref[_rows(j), cols]
            hloc = a * hloc + u_ref[_rows(j), cols]
            prod = a * prod
            hl_ref[_rows(j), cols] = hloc
            pl_ref[_rows(j), cols] = prod

        cur = jnp.where(seq_start, 0.0, hc_ref[0:1, cols])
        last = (J - 1) * SUBLANES
        for s in range(SUBLANES):
            c_ref[s:s + 1, cols] = cur
            cur = hl_ref[last + s:last + s + 1, cols] + pl_ref[last + s:last + s + 1, cols] * cur
        hc_ref[0:1, cols] = cur
        cvec = c_ref[:, cols]

        for j in range(J):
            hfull = hl_ref[_rows(j), cols] + pl_ref[_rows(j), cols] * cvec
            y_rnn = jax.nn.gelu(z_ref[_rows(j), zcols(o_ry, cols)]) * hfull
            yc = c3w_ref[_rows(0), cols] * q_ref[_rows(j), cols]
            for k in range(1, CONV3_WIDTH):
                yc = yc + c3w_ref[_rows(k), cols] * q_ref[_rows(j + k), cols]
            y_conv = z_ref[_rows(j), zcols(o_cb, cols)] * yc
            mrg_ref[_rows(j), cols] = (
                _sigmoid(z_ref[_rows(j), zcols(o_gr, cols)]) * y_rnn
                + _sigmoid(z_ref[_rows(j), zcols(o_gc, cols)]) * y_conv)

    def output(h, cols):
        w_rows = slice(cols.start // 2, cols.stop // 2)
        return jnp.dot(mrg_ref[:, cols].astype(jnp.bfloat16), _weights(wout_ref[w_rows, :]),
                       preferred_element_type=jnp.float32)

    starts = [(groups * g + k) % groups_per_seq == 0 for k in range(groups + 1)]
    slots = ((hn0_ref, z0_ref, xc0_ref, q0_ref), (hn1_ref, z1_ref, xc1_ref, q1_ref))

    @pl.when(g == 0)
    def _():
        p4_ref[...] = jnp.zeros_like(p4_ref)
        pq_ref[...] = jnp.zeros_like(pq_ref)
        hc_ref[...] = jnp.zeros_like(hc_ref)
        normalize(x_ref[0:SUB, :], hn0_ref)
        for chunk in range(len(chunks)):
            project(hn0_ref, z0_ref, chunk)
        for cols in heads:
            convs(z0_ref, xc0_ref, q0_ref, cols, True)

    for k in range(groups):
        rows = slice(k * SUB, (k + 1) * SUB)
        nxt = slice((k + 1) * SUB, (k + 2) * SUB)
        _, z_cur, xc_cur, q_cur = slots[k % 2]
        hn_nxt, z_nxt, xc_nxt, q_nxt = slots[(k + 1) % 2]
        normalize(x_ref[nxt, :] if k + 1 < groups else xn_ref[...], hn_nxt)
        acc = x_ref[rows, :]
        for h, cols in enumerate(heads):
            gates(xc_cur, h, cols)
            project(hn_nxt, z_nxt, h)
            scan_merge(z_cur, q_cur, cols, starts[k])
            if h > 0:
                acc = acc + output(h - 1, heads[h - 1])
        o_ref[rows, :] = acc + output(RG_HEADS - 1, heads[-1])
        for cols in heads:
            convs(z_nxt, xc_nxt, q_nxt, cols, starts[k + 1])


def _unpermute_exact(xp, unperm):
    f32, bf16 = jnp.float32, jnp.bfloat16
    hi = xp.astype(bf16)
    rest = xp - hi.astype(f32)
    mid = rest.astype(bf16)
    lo = (rest - mid.astype(f32)).astype(bf16)
    move = lambda piece: jnp.dot(unperm, piece, preferred_element_type=f32)
    return (move(hi) + move(mid)) + move(lo)


def _ffn_kernel(x_ref, xn_ref, p_ref, gffn_ref, wgu_ref, wd_ref, gple_ref, wpg_ref,
                wple_ref, gfin_ref, unperm_ref, o_ref, gu_ref, act0_ref, act1_ref, *,
                groups, d_ff, final):
    g = pl.program_id(0)

    def project(xs):
        hn = _rmsnorm(xs, gffn_ref[...]).astype(jnp.bfloat16)
        gu_ref[...] = jnp.dot(hn, _weights(wgu_ref[...]), preferred_element_type=jnp.float32)

    def activate(act_ref):
        gate = gu_ref[:, 0:d_ff]
        act_ref[...] = ((gate * _sigmoid(gate)) * gu_ref[:, d_ff:2 * d_ff]).astype(jnp.bfloat16)

    def down(act_ref, xs):
        return xs + jnp.dot(act_ref[...], _weights(wd_ref[...]), preferred_element_type=jnp.float32)

    def embed(x2, ps):
        hn2 = _rmsnorm(x2, gple_ref[...]).astype(jnp.bfloat16)
        gate = _sigmoid(jnp.dot(hn2, _weights(wpg_ref[...]), preferred_element_type=jnp.float32))
        pe = jnp.dot(ps.astype(jnp.bfloat16), _weights(wple_ref[...]), preferred_element_type=jnp.float32)
        x3 = x2 + gate * pe
        if final:
            x3 = _rmsnorm(x3, gfin_ref[...])
            x3 = jnp.swapaxes(x3.reshape(J, SUBLANES, -1), 0, 1).reshape(FFN_SUB, -1)
        return x3

    @pl.when(g == 0)
    def _():
        project(x_ref[0:FFN_SUB, :])
        activate(act0_ref)

    act_slots = (act0_ref, act1_ref)
    for k in range(groups):
        rows = slice(k * FFN_SUB, (k + 1) * FFN_SUB)
        nxt = slice((k + 1) * FFN_SUB, (k + 2) * FFN_SUB)
        x2 = down(act_slots[k % 2], x_ref[rows, :])
        project(x_ref[nxt, :] if k + 1 < groups else xn_ref[...])
        activate(act_slots[(k + 1) % 2])
        o_ref[rows, :] = embed(x2, p_ref[rows, :])


def _resident(shape):
    zeros = (0,) * len(shape)
    return pl.BlockSpec(shape, lambda g: zeros, pipeline_mode=pl.Buffered(1))


def _resident_layer(stacked_shape, layer):
    index = (layer,) + (0,) * (len(stacked_shape) - 1)
    return pl.BlockSpec((None,) + tuple(stacked_shape[1:]), lambda g: index,
                        pipeline_mode=pl.Buffered(1))


def _row_specs(n, d, sub, groups):
    last_group = n // sub - 1
    cur = pl.BlockSpec((groups * sub, d), lambda g: (g, 0))
    nxt = pl.BlockSpec((sub, d), lambda g: (jnp.minimum(groups * (g + 1), last_group), 0))
    return cur, nxt


def _mixer_call(x2d, gmix, win, c4w, c4b, wr, br, wi, bi, lam, c3w, wout, *, layer, seq_len):
    n, d = x2d.shape
    w_in_cols = win.shape[-1]
    f32 = jnp.float32
    groups = MIXER_GROUPS
    cur, nxt = _row_specs(n, d, SUB, groups)
    consts = (gmix, win, c4w, c4b, wr, br, wi, bi, lam, c3w, wout)
    stacked = (win, wr, wi, wout)
    return pl.pallas_call(
        functools.partial(_mixer_kernel, groups=groups, groups_per_seq=seq_len // SUB, d=d),
        grid=(n // (groups * SUB),),
        in_specs=[cur, nxt] + [
            _resident_layer(c.shape, layer) if any(c is s for s in stacked)
            else _resident(c.shape) for c in consts],
        out_specs=cur,
        out_shape=jax.ShapeDtypeStruct((n, d), f32),
        scratch_shapes=[
            pltpu.VMEM((SUB, d), jnp.bfloat16),
            pltpu.VMEM((SUB, d), jnp.bfloat16),
            pltpu.VMEM((SUB, w_in_cols), f32),
            pltpu.VMEM((SUB, w_in_cols), f32),
            pltpu.VMEM((SUB, d), f32),
            pltpu.VMEM((SUB, d), f32),
            pltpu.VMEM((SUB + _H3 * SUBLANES, d), f32),
            pltpu.VMEM((SUB + _H3 * SUBLANES, d), f32),
            pltpu.VMEM((2, SUB, d), f32),
            pltpu.VMEM((2, SUB, d), f32),
            pltpu.VMEM((SUB, d), f32),
            pltpu.VMEM((SUB, d), f32),
            pltpu.VMEM((SUB, d), f32),
            pltpu.VMEM((SUB, d), f32),
            pltpu.VMEM((SUB, d), f32),
            pltpu.VMEM((_H4 * SUBLANES, d), f32),
            pltpu.VMEM((_H3 * SUBLANES, d), f32),
            pltpu.VMEM((SUBLANES, d), f32),
            pltpu.VMEM((SUBLANES, d), f32),
        ],
        compiler_params=pltpu.CompilerParams(
            dimension_semantics=("arbitrary",), vmem_limit_bytes=VMEM_LIMIT_BYTES),
        name="mixer",
    )(x2d, x2d, *consts)


def _ffn_call(x2d, p2d, gffn, wgu, wd, gple, wpg, wple, gfin, unperm, *, layer, final):
    n, d = x2d.shape
    d_ff = wgu.shape[-1] // 2
    groups, sub = FFN_GROUPS, FFN_SUB
    cur, nxt = _row_specs(n, d, sub, groups)
    p_spec = pl.BlockSpec((groups * sub, p2d.shape[1]), lambda g: (g, 0))
    consts = (gffn, wgu, wd, gple, wpg, wple, gfin, unperm)
    stacked = (wgu, wd, wpg, wple)
    return pl.pallas_call(
        functools.partial(_ffn_kernel, groups=groups, d_ff=d_ff, final=final),
        grid=(n // (groups * sub),),
        in_specs=[cur, nxt, p_spec] + [
            _resident_layer(c.shape, layer) if any(c is s for s in stacked)
            else _resident(c.shape) for c in consts],
        out_specs=cur,
        out_shape=jax.ShapeDtypeStruct((n, d), jnp.float32),
        scratch_shapes=[pltpu.VMEM((sub, 2 * d_ff), jnp.float32),
                        pltpu.VMEM((sub, d_ff), jnp.bfloat16),
                        pltpu.VMEM((sub, d_ff), jnp.bfloat16)],
        compiler_params=pltpu.CompilerParams(
            dimension_semantics=("arbitrary",), vmem_limit_bytes=VMEM_LIMIT_BYTES),
        name="ffn_ple",
    )(x2d, x2d, p2d, *consts)


def kernel(x, p, g_mix, w_in, conv4_w, conv4_b, w_rg_r, b_rg_r, w_rg_i, b_rg_i, lru_lambda, conv3_w, w_out, g_ffn, w_gate_up, w_down, g_ple, w_ple_gate, w_ple, g_final):
    bsz, seq_len, d = x.shape
    depth = p.shape[0]
    for groups, sub in ((MIXER_GROUPS, SUB), (FFN_GROUPS, FFN_SUB)):
        assert groups % 2 == 0 and seq_len % (groups * sub) == 0
    assert d % (RG_HEADS * LANES) == 0 and FFN_SUB == SUB
    row = lambda v: v.reshape(1, -1)
    rows8 = lambda v: jnp.repeat(v.reshape(-1, v.shape[-1]), SUBLANES, axis=0)

    def pack_heads(w):
        hb = w.shape[-1]
        return _pack_weights(w.reshape(-1, hb, hb)).reshape(depth, -1, hb // 2, hb)

    xs = _permute_rows(x.reshape(bsz * seq_len, d))
    win_p, wout_p = _pack_weights(w_in), _pack_weights(w_out)
    wr_p, wi_p = pack_heads(w_rg_r), pack_heads(w_rg_i)
    wgu_p, wd_p = _pack_weights(w_gate_up), _pack_weights(w_down)
    wpg_p, wple_p = _pack_weights(w_ple_gate), _pack_weights(w_ple)

    t = np.arange(SUB)
    unperm = jnp.asarray((t[:, None] % J) * SUBLANES + t[:, None] // J == t[None, :], jnp.bfloat16)

    for i in range(depth):
        xs = _mixer_call(
            xs, row(g_mix[i]), win_p, rows8(conv4_w[i]), rows8(conv4_b[i]),
            wr_p, rows8(b_rg_r[i].reshape(-1)), wi_p,
            rows8(b_rg_i[i].reshape(-1)), rows8(lru_lambda[i]), rows8(conv3_w[i]),
            wout_p, layer=i, seq_len=seq_len)
        p_i = _permute_rows(p[i].reshape(bsz * seq_len, -1))
        xs = _ffn_call(
            xs, p_i, row(g_ffn[i]), wgu_p, wd_p,
            row(g_ple[i]), wpg_p, wple_p,
            row(g_final), unperm, layer=i, final=(i == depth - 1))
    return xs.reshape(bsz, seq_len, d)
```

```python
import functools

import jax
import jax.numpy as jnp
from jax import lax
from jax.experimental import pallas as pl
from jax.experimental.pallas import tpu as pltpu

EPS = 1e-6
RG_C = 8.0
RG_HEADS = 4
CONV4_WIDTH = 4
CONV3_WIDTH = 3

SUBLANES = 8
LANES = 128
PACK_BLOCK_ROWS = 512
PACK_BLOCK_COLS = 2048
SUB = 128
J = SUB // SUBLANES
MIXER_GROUPS = 2
FFN_GROUPS = 4
FFN_SUB = 128
VMEM_LIMIT_BYTES = 60000 * 1024

_H3 = CONV3_WIDTH - 1
_H4 = CONV4_WIDTH - 1


def _to_stream_order(a):
    return jnp.swapaxes(a.reshape(SUBLANES, J, a.shape[-1]), 0, 1).reshape(a.shape)


def _to_time_order(a):
    return jnp.swapaxes(a.reshape(J, SUBLANES, a.shape[-1]), 0, 1).reshape(a.shape)


def _rmsnorm(xs, g):
    ms = jnp.mean(xs * xs, axis=-1, keepdims=True)
    return (xs * lax.rsqrt(ms + EPS)) * g


def _sigmoid(z):
    return 0.5 * jnp.tanh(0.5 * z) + 0.5


def _softplus(z):
    return jnp.maximum(z, 0.0) + jnp.log1p(jnp.exp(-jnp.abs(z)))


def _rows(j, n=1):
    return slice(j * SUBLANES, (j + n) * SUBLANES)


def _largest_block(size, unit, cap):
    return max(b for b in range(unit, min(size, cap) + 1, unit) if size % b == 0)


def _pack_kernel(w_ref, o_ref):
    o_ref[...] = pltpu.bitcast(w_ref[...].astype(jnp.bfloat16), jnp.uint32)


def _pack_weights(w):
    layers, k, n = w.shape
    bk = _largest_block(k, 2 * SUBLANES, PACK_BLOCK_ROWS)
    bn = _largest_block(n, LANES, PACK_BLOCK_COLS)
    return pl.pallas_call(
        _pack_kernel,
        grid=(layers, k // bk, n // bn),
        in_specs=[pl.BlockSpec((None, bk, bn), lambda l, i, j: (l, i, j))],
        out_specs=pl.BlockSpec((None, bk // 2, bn), lambda l, i, j: (l, i, j)),
        out_shape=jax.ShapeDtypeStruct((layers, k // 2, n), jnp.uint32),
        name="pack_weights",
    )(w)


def _weights(packed):
    return pltpu.bitcast(packed, jnp.bfloat16)


def _mixer_kernel(x_ref, xn_ref, gmix_ref, win_ref, c4w_ref, c4b_ref, wr_ref, br_ref,
                  wi_ref, bi_ref, lam_ref, c3w_ref, wout_ref, o_ref,
                  hn0_ref, hn1_ref, z0_ref, z1_ref, xc0_ref, xc1_ref, q0_ref, q1_ref,
                  rp_ref, ip_ref, a_ref, u_ref, hl_ref, pl_ref, mrg_ref,
                  p4_ref, pq_ref, hc_ref, c_ref, xp0_ref, xp1_ref, *,
                  groups, groups_per_seq, d, permute_in):
    g = pl.program_id(0)
    hb = d // RG_HEADS
    o_rx, o_ry, o_cb, o_cc, o_cx, o_gr, o_gc = (k * d for k in range(7))
    first_stream = lax.broadcasted_iota(jnp.int32, (SUBLANES, hb), 0) == 0
    heads = [slice(h * hb, (h + 1) * hb) for h in range(RG_HEADS)]
    chunks = [(o_rx, o_ry), (o_cc, o_gr), (o_ry, o_cc), (o_gr, o_gc + d)]

    def zcols(off, cols):
        return slice(off + cols.start, off + cols.stop)

    def from_prev_stream(cur, prev):
        return jnp.where(first_stream, pltpu.roll(prev, 1, 0), pltpu.roll(cur, 1, 0))

    def normalize(xs, hn_ref):
        hn_ref[...] = _rmsnorm(xs, gmix_ref[...]).astype(jnp.bfloat16)

    def project(hn_ref, z_ref, chunk):
        lo, hi = chunks[chunk]
        z_ref[:, lo:hi] = jnp.dot(hn_ref[...], _weights(win_ref[:, lo:hi]),
                                  preferred_element_type=jnp.float32)

    def convs(z_ref, xc_ref, q_ref, cols, seq_start):
        def history(h_ref, jj):
            return jnp.where(seq_start, 0.0, h_ref[_rows(jj), cols])

        c_rx = zcols(o_rx, cols)
        halo4 = [from_prev_stream(z_ref[_rows(J - _H4 + jj), c_rx], history(p4_ref, jj))
                 for jj in range(_H4)]
        p4_ref[:, cols] = z_ref[_rows(J - _H4, _H4), c_rx]

        def rx(j):
            return halo4[j + _H4] if j < 0 else z_ref[_rows(j), c_rx]

        for j in range(J):
            acc = c4b_ref[:, cols] + c4w_ref[_rows(0), cols] * rx(j - _H4)
            for k in range(1, CONV4_WIDTH):
                acc = acc + c4w_ref[_rows(k), cols] * rx(j - _H4 + k)
            xc_ref[_rows(j), cols] = acc

        for j in range(J):
            q_ref[_rows(_H3 + j), cols] = (z_ref[_rows(j), zcols(o_cc, cols)]
                                           * z_ref[_rows(j), zcols(o_cx, cols)])
        for jj in range(_H3):
            q_ref[_rows(jj), cols] = from_prev_stream(q_ref[_rows(J + jj), cols],
                                                      history(pq_ref, jj))
        pq_ref[:, cols] = q_ref[_rows(J, _H3), cols]

    def gates(xc_ref, h, cols):
        xh = xc_ref[:, cols]
        xb = xh.astype(jnp.bfloat16)
        slot = lax.rem(g, 2)
        rp_ref[slot, :, cols] = jnp.dot(xb, _weights(wr_ref[h]), preferred_element_type=jnp.float32)
        ip_ref[slot, :, cols] = jnp.dot(xb, _weights(wi_ref[h]), preferred_element_type=jnp.float32)
        log_a_scale = -RG_C * _softplus(-lam_ref[:, cols])
        for j in range(J):
            r = _sigmoid(rp_ref[slot, _rows(j), cols] + br_ref[:, cols])
            gi = _sigmoid(ip_ref[slot, _rows(j), cols] + bi_ref[:, cols])
            log_a = log_a_scale * r
            a = jnp.exp(log_a)
            mult = jnp.sqrt(-jnp.tanh(log_a) * (1.0 + a * a))
            a_ref[_rows(j), cols] = a
            u_ref[_rows(j), cols] = mult * (gi * xh[_rows(j), :])

    def scan_merge(z_ref, q_ref, cols, seq_start):
        hloc = jnp.zeros((SUBLANES, hb), jnp.float32)
        prod = jnp.ones((SUBLANES, hb), jnp.float32)
        for j in range(J):
            a = a_ref[_rows(j), cols]
            hloc = a * hloc + u_ref[_rows(j), cols]
            prod = a * prod
            hl_ref[_rows(j), cols] = hloc
            pl_ref[_rows(j), cols] = prod

        cur = jnp.where(seq_start, 0.0, hc_ref[0:1, cols])
        last = (J - 1) * SUBLANES
        for s in range(SUBLANES):
            c_ref[s:s + 1, cols] = cur
            cur = hl_ref[last + s:last + s + 1, cols] + pl_ref[last + s:last + s + 1, cols] * cur
        hc_ref[0:1, cols] = cur
        cvec = c_ref[:, cols]

        for j in range(J):
            hfull = hl_ref[_rows(j), cols] + pl_ref[_rows(j), cols] * cvec
            y_rnn = jax.nn.gelu(z_ref[_rows(j), zcols(o_ry, cols)]) * hfull
            yc = c3w_ref[_rows(0), cols] * q_ref[_rows(j), cols]
            for k in range(1, CONV3_WIDTH):
                yc = yc + c3w_ref[_rows(k), cols] * q_ref[_rows(j + k), cols]
            y_conv = z_ref[_rows(j), zcols(o_cb, cols)] * yc
            mrg_ref[_rows(j), cols] = (
                _sigmoid(z_ref[_rows(j), zcols(o_gr, cols)]) * y_rnn
                + _sigmoid(z_ref[_rows(j), zcols(o_gc, cols)]) * y_conv)

    def output(h, cols):
        w_rows = slice(cols.start // 2, cols.stop // 2)
        return jnp.dot(mrg_ref[:, cols].astype(jnp.bfloat16), _weights(wout_ref[w_rows, :]),
                       preferred_element_type=jnp.float32)

    starts = [(groups * g + k) % groups_per_seq == 0 for k in range(groups + 1)]
    slots = ((hn0_ref, z0_ref, xc0_ref, q0_ref), (hn1_ref, z1_ref, xc1_ref, q1_ref))
    xp_slots = (xp0_ref, xp1_ref)

    def group_rows(k):
        xs = x_ref[k * SUB:(k + 1) * SUB, :] if k < groups else xn_ref[...]
        if not permute_in:
            return xs
        xp_ref = xp_slots[k % 2]
        xp_ref[...] = _to_stream_order(xs)
        return xp_ref[...]

    @pl.when(g == 0)
    def _():
        p4_ref[...] = jnp.zeros_like(p4_ref)
        pq_ref[...] = jnp.zeros_like(pq_ref)
        hc_ref[...] = jnp.zeros_like(hc_ref)
        normalize(group_rows(0), hn0_ref)
        for chunk in range(len(chunks)):
            project(hn0_ref, z0_ref, chunk)
        for cols in heads:
            convs(z0_ref, xc0_ref, q0_ref, cols, True)

    for k in range(groups):
        rows = slice(k * SUB, (k + 1) * SUB)
        _, z_cur, xc_cur, q_cur = slots[k % 2]
        hn_nxt, z_nxt, xc_nxt, q_nxt = slots[(k + 1) % 2]
        normalize(group_rows(k + 1), hn_nxt)
        acc = xp_slots[k % 2][...] if permute_in else x_ref[rows, :]
        for h, cols in enumerate(heads):
            gates(xc_cur, h, cols)
            project(hn_nxt, z_nxt, h)
            scan_merge(z_cur, q_cur, cols, starts[k])
            if h > 0:
                acc = acc + output(h - 1, heads[h - 1])
        o_ref[rows, :] = acc + output(RG_HEADS - 1, heads[-1])
        for cols in heads:
            convs(z_nxt, xc_nxt, q_nxt, cols, starts[k + 1])


def _ffn_kernel(x_ref, xn_ref, p_ref, gffn_ref, wgu_ref, wd_ref, gple_ref, wpg_ref,
                wple_ref, gfin_ref, o_ref, gu_ref, act0_ref, act1_ref, *,
                groups, d_ff, final):
    g = pl.program_id(0)

    def project(xs):
        hn = _rmsnorm(xs, gffn_ref[...]).astype(jnp.bfloat16)
        gu_ref[...] = jnp.dot(hn, _weights(wgu_ref[...]), preferred_element_type=jnp.float32)

    def activate(act_ref):
        gate = gu_ref[:, 0:d_ff]
        act_ref[...] = ((gate * _sigmoid(gate)) * gu_ref[:, d_ff:2 * d_ff]).astype(jnp.bfloat16)

    def down(act_ref, xs):
        return xs + jnp.dot(act_ref[...], _weights(wd_ref[...]), preferred_element_type=jnp.float32)

    def embed(x2, ps):
        hn2 = _rmsnorm(x2, gple_ref[...]).astype(jnp.bfloat16)
        gate = _sigmoid(jnp.dot(hn2, _weights(wpg_ref[...]), preferred_element_type=jnp.float32))
        pb = _to_stream_order(ps).astype(jnp.bfloat16)
        pe = jnp.dot(pb, _weights(wple_ref[...]), preferred_element_type=jnp.float32)
        x3 = x2 + gate * pe
        if final:
            x3 = _to_time_order(_rmsnorm(x3, gfin_ref[...]))
        return x3

    @pl.when(g == 0)
    def _():
        project(x_ref[0:FFN_SUB, :])
        activate(act0_ref)

    act_slots = (act0_ref, act1_ref)
    for k in range(groups):
        rows = slice(k * FFN_SUB, (k + 1) * FFN_SUB)
        nxt = slice((k + 1) * FFN_SUB, (k + 2) * FFN_SUB)
        x2 = down(act_slots[k % 2], x_ref[rows, :])
        project(x_ref[nxt, :] if k + 1 < groups else xn_ref[...])
        activate(act_slots[(k + 1) % 2])
        o_ref[rows, :] = embed(x2, p_ref[rows, :])


def _resident(shape):
    zeros = (0,) * len(shape)
    return pl.BlockSpec(shape, lambda g: zeros, pipeline_mode=pl.Buffered(1))


def _resident_layer(stacked_shape, layer):
    index = (layer,) + (0,) * (len(stacked_shape) - 1)
    return pl.BlockSpec((None,) + tuple(stacked_shape[1:]), lambda g: index,
                        pipeline_mode=pl.Buffered(1))


def _row_specs(n, d, sub, groups):
    last_group = n // sub - 1
    cur = pl.BlockSpec((groups * sub, d), lambda g: (g, 0))
    nxt = pl.BlockSpec((sub, d), lambda g: (jnp.minimum(groups * (g + 1), last_group), 0))
    return cur, nxt


def _mixer_call(x2d, gmix, win, c4w, c4b, wr, br, wi, bi, lam, c3w, wout, *, layer, seq_len,
                permute_in):
    n, d = x2d.shape
    w_in_cols = win.shape[-1]
    f32 = jnp.float32
    groups = MIXER_GROUPS
    cur, nxt = _row_specs(n, d, SUB, groups)
    consts = (gmix, win, c4w, c4b, wr, br, wi, bi, lam, c3w, wout)
    stacked = (win, wr, wi, wout)
    return pl.pallas_call(
        functools.partial(_mixer_kernel, groups=groups, groups_per_seq=seq_len // SUB, d=d,
                          permute_in=permute_in),
        grid=(n // (groups * SUB),),
        in_specs=[cur, nxt] + [
            _resident_layer(c.shape, layer) if any(c is s for s in stacked)
            else _resident(c.shape) for c in consts],
        out_specs=cur,
        out_shape=jax.ShapeDtypeStruct((n, d), f32),
        scratch_shapes=[
            pltpu.VMEM((SUB, d), jnp.bfloat16),
            pltpu.VMEM((SUB, d), jnp.bfloat16),
            pltpu.VMEM((SUB, w_in_cols), f32),
            pltpu.VMEM((SUB, w_in_cols), f32),
            pltpu.VMEM((SUB, d), f32),
            pltpu.VMEM((SUB, d), f32),
            pltpu.VMEM((SUB + _H3 * SUBLANES, d), f32),
            pltpu.VMEM((SUB + _H3 * SUBLANES, d), f32),
            pltpu.VMEM((2, SUB, d), f32),
            pltpu.VMEM((2, SUB, d), f32),
            pltpu.VMEM((SUB, d), f32),
            pltpu.VMEM((SUB, d), f32),
            pltpu.VMEM((SUB, d), f32),
            pltpu.VMEM((SUB, d), f32),
            pltpu.VMEM((SUB, d), f32),
            pltpu.VMEM((_H4 * SUBLANES, d), f32),
            pltpu.VMEM((_H3 * SUBLANES, d), f32),
            pltpu.VMEM((SUBLANES, d), f32),
            pltpu.VMEM((SUBLANES, d), f32),
            pltpu.VMEM((SUB, d), f32),
            pltpu.VMEM((SUB, d), f32),
        ],
        compiler_params=pltpu.CompilerParams(
            dimension_semantics=("arbitrary",), vmem_limit_bytes=VMEM_LIMIT_BYTES),
        name="mixer",
    )(x2d, x2d, *consts)


def _ffn_call(x2d, p3d, gffn, wgu, wd, gple, wpg, wple, gfin, *, layer, final):
    n, d = x2d.shape
    d_ff = wgu.shape[-1] // 2
    groups, sub = FFN_GROUPS, FFN_SUB
    cur, nxt = _row_specs(n, d, sub, groups)
    p_spec = pl.BlockSpec((None, groups * sub, p3d.shape[-1]), lambda g: (layer, g, 0))
    consts = (gffn, wgu, wd, gple, wpg, wple, gfin)
    stacked = (wgu, wd, wpg, wple)
    return pl.pallas_call(
        functools.partial(_ffn_kernel, groups=groups, d_ff=d_ff, final=final),
        grid=(n // (groups * sub),),
        in_specs=[cur, nxt, p_spec] + [
            _resident_layer(c.shape, layer) if any(c is s for s in stacked)
            else _resident(c.shape) for c in consts],
        out_specs=cur,
        out_shape=jax.ShapeDtypeStruct((n, d), jnp.float32),
        scratch_shapes=[pltpu.VMEM((sub, 2 * d_ff), jnp.float32),
                        pltpu.VMEM((sub, d_ff), jnp.bfloat16),
                        pltpu.VMEM((sub, d_ff), jnp.bfloat16)],
        compiler_params=pltpu.CompilerParams(
            dimension_semantics=("arbitrary",), vmem_limit_bytes=VMEM_LIMIT_BYTES),
        name="ffn_ple",
    )(x2d, x2d, p3d, *consts)


def kernel(x, p, g_mix, w_in, conv4_w, conv4_b, w_rg_r, b_rg_r, w_rg_i, b_rg_i, lru_lambda, conv3_w, w_out, g_ffn, w_gate_up, w_down, g_ple, w_ple_gate, w_ple, g_final):
    bsz, seq_len, d = x.shape
    depth = p.shape[0]
    for groups, sub in ((MIXER_GROUPS, SUB), (FFN_GROUPS, FFN_SUB)):
        assert groups % 2 == 0 and seq_len % (groups * sub) == 0
    assert d % (RG_HEADS * LANES) == 0 and FFN_SUB == SUB
    row = lambda v: v.reshape(1, -1)
    rows8 = lambda v: jnp.repeat(v.reshape(-1, v.shape[-1]), SUBLANES, axis=0)

    def pack_heads(w):
        hb = w.shape[-1]
        return _pack_weights(w.reshape(-1, hb, hb)).reshape(depth, -1, hb // 2, hb)

    xs = x.reshape(bsz * seq_len, d)
    win_p, wout_p = _pack_weights(w_in), _pack_weights(w_out)
    wr_p, wi_p = pack_heads(w_rg_r), pack_heads(w_rg_i)
    wgu_p, wd_p = _pack_weights(w_gate_up), _pack_weights(w_down)
    wpg_p, wple_p = _pack_weights(w_ple_gate), _pack_weights(w_ple)

    p3d = p.reshape(depth, bsz * seq_len, -1)

    for i in range(depth):
        xs = _mixer_call(
            xs, row(g_mix[i]), win_p, rows8(conv4_w[i]), rows8(conv4_b[i]),
            wr_p, rows8(b_rg_r[i].reshape(-1)), wi_p,
            rows8(b_rg_i[i].reshape(-1)), rows8(lru_lambda[i]), rows8(conv3_w[i]),
            wout_p, layer=i, seq_len=seq_len, permute_in=(i == 0))
        xs = _ffn_call(
            xs, p3d, row(g_ffn[i]), wgu_p, wd_p,
            row(g_ple[i]), wpg_p, wple_p,
            row(g_final), layer=i, final=(i == depth - 1))
    return xs.reshape(bsz, seq_len, d)
```

```python
import functools

import jax
import jax.numpy as jnp
from jax import lax
from jax.experimental import pallas as pl
from jax.experimental.pallas import tpu as pltpu

EPS = 1e-6
RG_C = 8.0
RG_HEADS = 4
CONV4_WIDTH = 4
CONV3_WIDTH = 3

SUBLANES = 8
LANES = 128
PACK_BLOCK_ROWS = 512
PACK_BLOCK_COLS = 2048
SUB = 128
J = SUB // SUBLANES
MIXER_GROUPS = 2
FFN_GROUPS = 4
FFN_SUB = 128
VMEM_LIMIT_BYTES = 60000 * 1024

_H3 = CONV3_WIDTH - 1
_H4 = CONV4_WIDTH - 1


def _to_stream_order(a):
    return jnp.swapaxes(a.reshape(SUBLANES, J, a.shape[-1]), 0, 1).reshape(a.shape)


def _to_time_order(a):
    return jnp.swapaxes(a.reshape(J, SUBLANES, a.shape[-1]), 0, 1).reshape(a.shape)


def _rmsnorm(xs, g):
    ms = jnp.mean(xs * xs, axis=-1, keepdims=True)
    return (xs * lax.rsqrt(ms + EPS)) * g


def _sigmoid(z):
    return 0.5 * jnp.tanh(0.5 * z) + 0.5


def _softplus(z):
    return jnp.maximum(z, 0.0) + jnp.log1p(jnp.exp(-jnp.abs(z)))


def _rows(j, n=1):
    return slice(j * SUBLANES, (j + n) * SUBLANES)


def _largest_block(size, unit, cap):
    return max(b for b in range(unit, min(size, cap) + 1, unit) if size % b == 0)


def _pack_kernel(w_ref, o_ref):
    o_ref[...] = pltpu.bitcast(w_ref[...].astype(jnp.bfloat16), jnp.uint32)


def _pack_weights(w):
    layers, k, n = w.shape
    bk = _largest_block(k, 2 * SUBLANES, PACK_BLOCK_ROWS)
    bn = _largest_block(n, LANES, PACK_BLOCK_COLS)
    return pl.pallas_call(
        _pack_kernel,
        grid=(layers, k // bk, n // bn),
        in_specs=[pl.BlockSpec((None, bk, bn), lambda l, i, j: (l, i, j))],
        out_specs=pl.BlockSpec((None, bk // 2, bn), lambda l, i, j: (l, i, j)),
        out_shape=jax.ShapeDtypeStruct((layers, k // 2, n), jnp.uint32),
        name="pack_weights",
    )(w)


def _weights(packed):
    return pltpu.bitcast(packed, jnp.bfloat16)


def _mixer_kernel(x_ref, xn_ref, gmix_ref, win_ref, c4w_ref, c4b_ref, wri_ref, br_ref,
                  bi_ref, lam_ref, c3w_ref, wout_ref, o_ref,
                  hn0_ref, hn1_ref, z0_ref, z1_ref, xc0_ref, xc1_ref, q0_ref, q1_ref,
                  pre_ref, a_ref, u_ref, hl_ref, pl_ref, mrg_ref,
                  p4_ref, pq_ref, hc_ref, c_ref, xp0_ref, xp1_ref, *,
                  groups, groups_per_seq, d, permute_in):
    g = pl.program_id(0)
    hb = d // RG_HEADS
    o_rx, o_ry, o_cb, o_cc, o_cx, o_gr, o_gc = (k * d for k in range(7))
    first_stream = lax.broadcasted_iota(jnp.int32, (SUBLANES, hb), 0) == 0
    heads = [slice(h * hb, (h + 1) * hb) for h in range(RG_HEADS)]
    chunks = [(o_rx, o_ry), (o_cc, o_gr), (o_ry, o_cc), (o_gr, o_gc + d)]

    def zcols(off, cols):
        return slice(off + cols.start, off + cols.stop)

    def from_prev_stream(cur, prev):
        return jnp.where(first_stream, pltpu.roll(prev, 1, 0), pltpu.roll(cur, 1, 0))

    def normalize(xs, hn_ref):
        hn_ref[...] = _rmsnorm(xs, gmix_ref[...]).astype(jnp.bfloat16)

    def project(hn_ref, z_ref, chunk):
        lo, hi = chunks[chunk]
        z_ref[:, lo:hi] = jnp.dot(hn_ref[...], _weights(win_ref[:, lo:hi]),
                                  preferred_element_type=jnp.float32)

    def convs(z_ref, xc_ref, q_ref, cols, seq_start):
        def history(h_ref, jj):
            return jnp.where(seq_start, 0.0, h_ref[_rows(jj), cols])

        c_rx = zcols(o_rx, cols)
        halo4 = [from_prev_stream(z_ref[_rows(J - _H4 + jj), c_rx], history(p4_ref, jj))
                 for jj in range(_H4)]
        p4_ref[:, cols] = z_ref[_rows(J - _H4, _H4), c_rx]

        def rx(j):
            return halo4[j + _H4] if j < 0 else z_ref[_rows(j), c_rx]

        for j in range(J):
            acc = c4b_ref[:, cols] + c4w_ref[_rows(0), cols] * rx(j - _H4)
            for k in range(1, CONV4_WIDTH):
                acc = acc + c4w_ref[_rows(k), cols] * rx(j - _H4 + k)
            xc_ref[_rows(j), cols] = acc

        for j in range(J):
            q_ref[_rows(_H3 + j), cols] = (z_ref[_rows(j), zcols(o_cc, cols)]
                                           * z_ref[_rows(j), zcols(o_cx, cols)])
        for jj in range(_H3):
            q_ref[_rows(jj), cols] = from_prev_stream(q_ref[_rows(J + jj), cols],
                                                      history(pq_ref, jj))
        pq_ref[:, cols] = q_ref[_rows(J, _H3), cols]

    def gates(xc_ref, h, cols):
        xh = xc_ref[:, cols]
        xb = xh.astype(jnp.bfloat16)
        slot = lax.rem(g, 2)
        r_cols = slice(2 * cols.start, 2 * cols.start + hb)
        i_cols = slice(2 * cols.start + hb, 2 * cols.stop)
        pre_ref[slot, :, 2 * cols.start:2 * cols.stop] = jnp.dot(
            xb, _weights(wri_ref[h]), preferred_element_type=jnp.float32)
        half_scale = (-0.5 * RG_C) * _softplus(-lam_ref[:, cols])
        for j in range(J):
            t_r = jnp.tanh(0.5 * (pre_ref[slot, _rows(j), r_cols] + br_ref[:, cols]))
            gi = _sigmoid(pre_ref[slot, _rows(j), i_cols] + bi_ref[:, cols])
            log_a = half_scale * t_r + half_scale
            a = jnp.exp(log_a)
            w = -jnp.tanh(log_a) * (1.0 + a * a)
            mult = jnp.where(w > 0.0, w * lax.rsqrt(w), 0.0)
            a_ref[_rows(j), cols] = a
            u_ref[_rows(j), cols] = mult * (gi * xh[_rows(j), :])

    def scan_merge(z_ref, q_ref, cols, seq_start):
        hloc = jnp.zeros((SUBLANES, hb), jnp.float32)
        prod = jnp.ones((SUBLANES, hb), jnp.float32)
        for j in range(J):
            a = a_ref[_rows(j), cols]
            hloc = a * hloc + u_ref[_rows(j), cols]
            prod = a * prod
            hl_ref[_rows(j), cols] = hloc
            pl_ref[_rows(j), cols] = prod

        cur = jnp.where(seq_start, 0.0, hc_ref[0:1, cols])
        last = (J - 1) * SUBLANES
        for s in range(SUBLANES):
            c_ref[s:s + 1, cols] = cur
            cur = hl_ref[last + s:last + s + 1, cols] + pl_ref[last + s:last + s + 1, cols] * cur
        hc_ref[0:1, cols] = cur
        cvec = c_ref[:, cols]

        for j in range(J):
            hfull = hl_ref[_rows(j), cols] + pl_ref[_rows(j), cols] * cvec
            y_rnn = jax.nn.gelu(z_ref[_rows(j), zcols(o_ry, cols)]) * hfull
            yc = c3w_ref[_rows(0), cols] * q_ref[_rows(j), cols]
            for k in range(1, CONV3_WIDTH):
                yc = yc + c3w_ref[_rows(k), cols] * q_ref[_rows(j + k), cols]
            y_conv = z_ref[_rows(j), zcols(o_cb, cols)] * yc
            mrg_ref[_rows(j), cols] = (
                _sigmoid(z_ref[_rows(j), zcols(o_gr, cols)]) * y_rnn
                + _sigmoid(z_ref[_rows(j), zcols(o_gc, cols)]) * y_conv)

    def output(h, cols):
        w_rows = slice(cols.start // 2, cols.stop // 2)
        return jnp.dot(mrg_ref[:, cols].astype(jnp.bfloat16), _weights(wout_ref[w_rows, :]),
                       preferred_element_type=jnp.float32)

    starts = [(groups * g + k) % groups_per_seq == 0 for k in range(groups + 1)]
    slots = ((hn0_ref, z0_ref, xc0_ref, q0_ref), (hn1_ref, z1_ref, xc1_ref, q1_ref))
    xp_slots = (xp0_ref, xp1_ref)

    def group_rows(k):
        xs = x_ref[k * SUB:(k + 1) * SUB, :] if k < groups else xn_ref[...]
        if not permute_in:
            return xs
        xp_ref = xp_slots[k % 2]
        xp_ref[...] = _to_stream_order(xs)
        return xp_ref[...]

    @pl.when(g == 0)
    def _():
        p4_ref[...] = jnp.zeros_like(p4_ref)
        pq_ref[...] = jnp.zeros_like(pq_ref)
        hc_ref[...] = jnp.zeros_like(hc_ref)
        normalize(group_rows(0), hn0_ref)
        for chunk in range(len(chunks)):
            project(hn0_ref, z0_ref, chunk)
        for cols in heads:
            convs(z0_ref, xc0_ref, q0_ref, cols, True)

    for k in range(groups):
        rows = slice(k * SUB, (k + 1) * SUB)
        _, z_cur, xc_cur, q_cur = slots[k % 2]
        hn_nxt, z_nxt, xc_nxt, q_nxt = slots[(k + 1) % 2]
        normalize(group_rows(k + 1), hn_nxt)
        acc = xp_slots[k % 2][...] if permute_in else x_ref[rows, :]
        for h, cols in enumerate(heads):
            gates(xc_cur, h, cols)
            project(hn_nxt, z_nxt, h)
            scan_merge(z_cur, q_cur, cols, starts[k])
            if h > 0:
                acc = acc + output(h - 1, heads[h - 1])
        o_ref[rows, :] = acc + output(RG_HEADS - 1, heads[-1])
        for cols in heads:
            convs(z_nxt, xc_nxt, q_nxt, cols, starts[k + 1])


def _ffn_kernel(x_ref, xn_ref, p_ref, gffn_ref, wgu_ref, wd_ref, gple_ref, wpg_ref,
                wple_ref, gfin_ref, o_ref, gu_ref, act0_ref, act1_ref, *,
                groups, d_ff, final):
    g = pl.program_id(0)

    def project(xs):
        hn = _rmsnorm(xs, gffn_ref[...]).astype(jnp.bfloat16)
        gu_ref[...] = jnp.dot(hn, _weights(wgu_ref[...]), preferred_element_type=jnp.float32)

    def activate(act_ref):
        gate = gu_ref[:, 0:d_ff]
        act_ref[...] = ((gate * _sigmoid(gate)) * gu_ref[:, d_ff:2 * d_ff]).astype(jnp.bfloat16)

    def down(act_ref, xs):
        return xs + jnp.dot(act_ref[...], _weights(wd_ref[...]), preferred_element_type=jnp.float32)

    def embed(x2, ps):
        hn2 = _rmsnorm(x2, gple_ref[...]).astype(jnp.bfloat16)
        gate = _sigmoid(jnp.dot(hn2, _weights(wpg_ref[...]), preferred_element_type=jnp.float32))
        pb = _to_stream_order(ps).astype(jnp.bfloat16)
        pe = jnp.dot(pb, _weights(wple_ref[...]), preferred_element_type=jnp.float32)
        x3 = x2 + gate * pe
        if final:
            x3 = _to_time_order(_rmsnorm(x3, gfin_ref[...]))
        return x3

    @pl.when(g == 0)
    def _():
        project(x_ref[0:FFN_SUB, :])
        activate(act0_ref)

    act_slots = (act0_ref, act1_ref)
    for k in range(groups):
        rows = slice(k * FFN_SUB, (k + 1) * FFN_SUB)
        nxt = slice((k + 1) * FFN_SUB, (k + 2) * FFN_SUB)
        x2 = down(act_slots[k % 2], x_ref[rows, :])
        project(x_ref[nxt, :] if k + 1 < groups else xn_ref[...])
        activate(act_slots[(k + 1) % 2])
        o_ref[rows, :] = embed(x2, p_ref[rows, :])


def _resident(shape):
    zeros = (0,) * len(shape)
    return pl.BlockSpec(shape, lambda g: zeros, pipeline_mode=pl.Buffered(1))


def _resident_layer(stacked_shape, layer):
    index = (layer,) + (0,) * (len(stacked_shape) - 1)
    return pl.BlockSpec((None,) + tuple(stacked_shape[1:]), lambda g: index,
                        pipeline_mode=pl.Buffered(1))


def _row_specs(n, d, sub, groups):
    last_group = n // sub - 1
    cur = pl.BlockSpec((groups * sub, d), lambda g: (g, 0))
    nxt = pl.BlockSpec((sub, d), lambda g: (jnp.minimum(groups * (g + 1), last_group), 0))
    return cur, nxt


def _mixer_call(x2d, gmix, win, c4w, c4b, wri, br, bi, lam, c3w, wout, *, layer, seq_len,
                permute_in):
    n, d = x2d.shape
    w_in_cols = win.shape[-1]
    f32 = jnp.float32
    groups = MIXER_GROUPS
    cur, nxt = _row_specs(n, d, SUB, groups)
    consts = (gmix, win, c4w, c4b, wri, br, bi, lam, c3w, wout)
    stacked = (win, wri, wout)
    return pl.pallas_call(
        functools.partial(_mixer_kernel, groups=groups, groups_per_seq=seq_len // SUB, d=d,
                          permute_in=permute_in),
        grid=(n // (groups * SUB),),
        in_specs=[cur, nxt] + [
            _resident_layer(c.shape, layer) if any(c is s for s in stacked)
            else _resident(c.shape) for c in consts],
        out_specs=cur,
        out_shape=jax.ShapeDtypeStruct((n, d), f32),
        scratch_shapes=[
            pltpu.VMEM((SUB, d), jnp.bfloat16),
            pltpu.VMEM((SUB, d), jnp.bfloat16),
            pltpu.VMEM((SUB, w_in_cols), f32),
            pltpu.VMEM((SUB, w_in_cols), f32),
            pltpu.VMEM((SUB, d), f32),
            pltpu.VMEM((SUB, d), f32),
            pltpu.VMEM((SUB + _H3 * SUBLANES, d), f32),
            pltpu.VMEM((SUB + _H3 * SUBLANES, d), f32),
            pltpu.VMEM((2, SUB, 2 * d), f32),
            pltpu.VMEM((SUB, d), f32),
            pltpu.VMEM((SUB, d), f32),
            pltpu.VMEM((SUB, d), f32),
            pltpu.VMEM((SUB, d), f32),
            pltpu.VMEM((SUB, d), f32),
            pltpu.VMEM((_H4 * SUBLANES, d), f32),
            pltpu.VMEM((_H3 * SUBLANES, d), f32),
            pltpu.VMEM((SUBLANES, d), f32),
            pltpu.VMEM((SUBLANES, d), f32),
            pltpu.VMEM((SUB, d), f32),
            pltpu.VMEM((SUB, d), f32),
        ],
        compiler_params=pltpu.CompilerParams(
            dimension_semantics=("arbitrary",), vmem_limit_bytes=VMEM_LIMIT_BYTES),
        name="mixer",
    )(x2d, x2d, *consts)


def _ffn_call(x2d, p3d, gffn, wgu, wd, gple, wpg, wple, gfin, *, layer, final):
    n, d = x2d.shape
    d_ff = wgu.shape[-1] // 2
    groups, sub = FFN_GROUPS, FFN_SUB
    cur, nxt = _row_specs(n, d, sub, groups)
    p_spec = pl.BlockSpec((None, groups * sub, p3d.shape[-1]), lambda g: (layer, g, 0))
    consts = (gffn, wgu, wd, gple, wpg, wple, gfin)
    stacked = (wgu, wd, wpg, wple)
    return pl.pallas_call(
        functools.partial(_ffn_kernel, groups=groups, d_ff=d_ff, final=final),
        grid=(n // (groups * sub),),
        in_specs=[cur, nxt, p_spec] + [
            _resident_layer(c.shape, layer) if any(c is s for s in stacked)
            else _resident(c.shape) for c in consts],
        out_specs=cur,
        out_shape=jax.ShapeDtypeStruct((n, d), jnp.float32),
        scratch_shapes=[pltpu.VMEM((sub, 2 * d_ff), jnp.float32),
                        pltpu.VMEM((sub, d_ff), jnp.bfloat16),
                        pltpu.VMEM((sub, d_ff), jnp.bfloat16)],
        compiler_params=pltpu.CompilerParams(
            dimension_semantics=("arbitrary",), vmem_limit_bytes=VMEM_LIMIT_BYTES),
        name="ffn_ple",
    )(x2d, x2d, p3d, *consts)


def kernel(x, p, g_mix, w_in, conv4_w, conv4_b, w_rg_r, b_rg_r, w_rg_i, b_rg_i, lru_lambda, conv3_w, w_out, g_ffn, w_gate_up, w_down, g_ple, w_ple_gate, w_ple, g_final):
    bsz, seq_len, d = x.shape
    depth = p.shape[0]
    for groups, sub in ((MIXER_GROUPS, SUB), (FFN_GROUPS, FFN_SUB)):
        assert groups % 2 == 0 and seq_len % (groups * sub) == 0
    assert d % (RG_HEADS * LANES) == 0 and FFN_SUB == SUB
    row = lambda v: v.reshape(1, -1)
    rows8 = lambda v: jnp.repeat(v.reshape(-1, v.shape[-1]), SUBLANES, axis=0)

    def pack_heads(w):
        hb = w.shape[-1]
        return _pack_weights(w.reshape(-1, hb, hb)).reshape(depth, -1, hb // 2, hb)

    xs = x.reshape(bsz * seq_len, d)
    win_p, wout_p = _pack_weights(w_in), _pack_weights(w_out)
    wri_p = jnp.concatenate([pack_heads(w_rg_r), pack_heads(w_rg_i)], axis=-1)
    wgu_p, wd_p = _pack_weights(w_gate_up), _pack_weights(w_down)
    wpg_p, wple_p = _pack_weights(w_ple_gate), _pack_weights(w_ple)

    p3d = p.reshape(depth, bsz * seq_len, -1)

    for i in range(depth):
        xs = _mixer_call(
            xs, row(g_mix[i]), win_p, rows8(conv4_w[i]), rows8(conv4_b[i]),
            wri_p, rows8(b_rg_r[i].reshape(-1)), rows8(b_rg_i[i].reshape(-1)), rows8(lru_lambda[i]), rows8(conv3_w[i]),
            wout_p, layer=i, seq_len=seq_len, permute_in=(i == 0))
        xs = _ffn_call(
            xs, p3d, row(g_ffn[i]), wgu_p, wd_p,
            row(g_ple[i]), wpg_p, wple_p,
            row(g_final), layer=i, final=(i == depth - 1))
    return xs.reshape(bsz, seq_len, d)
```

```python
import functools

import jax
import jax.numpy as jnp
from jax import lax
from jax.experimental import pallas as pl
from jax.experimental.pallas import tpu as pltpu

EPS = 1e-6
RG_C = 8.0
RG_HEADS = 4
CONV4_WIDTH = 4
CONV3_WIDTH = 3

SUBLANES = 8
LANES = 128
PACK_BLOCK_ROWS = 512
PACK_BLOCK_COLS = 2048
SUB = 128
J = SUB // SUBLANES
MIXER_GROUPS = 2
FFN_GROUPS = 4
FFN_SUB = 128
VMEM_LIMIT_BYTES = 60000 * 1024

_H3 = CONV3_WIDTH - 1
_H4 = CONV4_WIDTH - 1


def _to_stream_order(a):
    return jnp.swapaxes(a.reshape(SUBLANES, J, a.shape[-1]), 0, 1).reshape(a.shape)


def _to_time_order(a):
    return jnp.swapaxes(a.reshape(J, SUBLANES, a.shape[-1]), 0, 1).reshape(a.shape)


def _rmsnorm(xs, g):
    ms = jnp.mean(xs * xs, axis=-1, keepdims=True)
    return (xs * lax.rsqrt(ms + EPS)) * g


def _sigmoid(z):
    return 0.5 * jnp.tanh(0.5 * z) + 0.5


def _softplus(z):
    return jnp.maximum(z, 0.0) + jnp.log1p(jnp.exp(-jnp.abs(z)))


def _rows(j, n=1):
    return slice(j * SUBLANES, (j + n) * SUBLANES)


def _largest_block(size, unit, cap):
    return max(b for b in range(unit, min(size, cap) + 1, unit) if size % b == 0)


def _pack_kernel(w_ref, o_ref):
    o_ref[...] = pltpu.bitcast(w_ref[...].astype(jnp.bfloat16), jnp.uint32)


def _pack_weights(w):
    layers, k, n = w.shape
    bk = _largest_block(k, 2 * SUBLANES, PACK_BLOCK_ROWS)
    bn = _largest_block(n, LANES, PACK_BLOCK_COLS)
    return pl.pallas_call(
        _pack_kernel,
        grid=(layers, k // bk, n // bn),
        in_specs=[pl.BlockSpec((None, bk, bn), lambda l, i, j: (l, i, j))],
        out_specs=pl.BlockSpec((None, bk // 2, bn), lambda l, i, j: (l, i, j)),
        out_shape=jax.ShapeDtypeStruct((layers, k // 2, n), jnp.uint32),
        name="pack_weights",
    )(w)


def _weights(packed):
    return pltpu.bitcast(packed, jnp.bfloat16)


def _mixer_kernel(x_ref, xn_ref, gmix_ref, win_ref, c4w_ref, c4b_ref, wri_ref, br_ref,
                  bi_ref, lam_ref, c3w_ref, wout_ref, o_ref,
                  hn0_ref, hn1_ref, z0_ref, z1_ref, xc0_ref, xc1_ref, yc0_ref, yc1_ref,
                  pre_ref, a_ref, u_ref, hl_ref, pl_ref, mrg_ref,
                  p4_ref, pq_ref, hc_ref, c_ref, xp0_ref, xp1_ref, *,
                  groups, groups_per_seq, d, permute_in):
    g = pl.program_id(0)
    hb = d // RG_HEADS
    o_rx, o_ry, o_cb, o_cc, o_cx, o_gr, o_gc = (k * d for k in range(7))
    first_stream = lax.broadcasted_iota(jnp.int32, (SUBLANES, hb), 0) == 0
    heads = [slice(h * hb, (h + 1) * hb) for h in range(RG_HEADS)]
    chunks = [(o_rx, o_ry), (o_cc, o_gr), (o_ry, o_cc), (o_gr, o_gc + d)]

    def zcols(off, cols):
        return slice(off + cols.start, off + cols.stop)

    def from_prev_stream(cur, prev):
        return jnp.where(first_stream, pltpu.roll(prev, 1, 0), pltpu.roll(cur, 1, 0))

    def normalize(xs, hn_ref):
        hn_ref[...] = _rmsnorm(xs, gmix_ref[...]).astype(jnp.bfloat16)

    def project(hn_ref, z_ref, chunk):
        lo, hi = chunks[chunk]
        z_ref[:, lo:hi] = jnp.dot(hn_ref[...], _weights(win_ref[:, lo:hi]),
                                  preferred_element_type=jnp.float32)

    def convs(z_ref, xc_ref, yc_ref, cols, seq_start):
        def history(h_ref, jj):
            return jnp.where(seq_start, 0.0, h_ref[_rows(jj), cols])

        c_rx = zcols(o_rx, cols)
        halo4 = [from_prev_stream(z_ref[_rows(J - _H4 + jj), c_rx], history(p4_ref, jj))
                 for jj in range(_H4)]
        p4_ref[:, cols] = z_ref[_rows(J - _H4, _H4), c_rx]

        def rx(j):
            return halo4[j + _H4] if j < 0 else z_ref[_rows(j), c_rx]

        for j in range(J):
            acc = c4b_ref[:, cols] + c4w_ref[_rows(0), cols] * rx(j - _H4)
            for k in range(1, CONV4_WIDTH):
                acc = acc + c4w_ref[_rows(k), cols] * rx(j - _H4 + k)
            xc_ref[_rows(j), cols] = acc

        def q(j):
            return z_ref[_rows(j), zcols(o_cc, cols)] * z_ref[_rows(j), zcols(o_cx, cols)]

        tail = [q(J - _H3 + jj) for jj in range(_H3)]
        window = [from_prev_stream(tail[jj], history(pq_ref, jj)) for jj in range(_H3)]
        for jj in range(_H3):
            pq_ref[_rows(jj), cols] = tail[jj]
        for j in range(J):
            qj = q(j) if j < J - _H3 else tail[j - (J - _H3)]
            yc = c3w_ref[_rows(_H3), cols] * qj
            for k in range(_H3):
                yc = yc + c3w_ref[_rows(k), cols] * window[k]
            yc_ref[_rows(j), cols] = yc
            window = window[1:] + [qj]

    def gates(xc_ref, h, cols):
        xh = xc_ref[:, cols]
        xb = xh.astype(jnp.bfloat16)
        slot = lax.rem(g, 2)
        r_cols = slice(2 * cols.start, 2 * cols.start + hb)
        i_cols = slice(2 * cols.start + hb, 2 * cols.stop)
        pre_ref[slot, :, 2 * cols.start:2 * cols.stop] = jnp.dot(
            xb, _weights(wri_ref[h]), preferred_element_type=jnp.float32)
        half_scale = (-0.5 * RG_C) * _softplus(-lam_ref[:, cols])
        for j in range(J):
            t_r = jnp.tanh(0.5 * (pre_ref[slot, _rows(j), r_cols] + br_ref[:, cols]))
            gi = _sigmoid(pre_ref[slot, _rows(j), i_cols] + bi_ref[:, cols])
            log_a = half_scale * t_r + half_scale
            a = jnp.exp(log_a)
            w = -jnp.tanh(log_a) * (1.0 + a * a)
            mult = jnp.where(w > 0.0, w * lax.rsqrt(w), 0.0)
            a_ref[_rows(j), cols] = a
            u_ref[_rows(j), cols] = mult * (gi * xh[_rows(j), :])

    def scan_merge(z_ref, yc_ref, cols, seq_start):
        hloc = jnp.zeros((SUBLANES, hb), jnp.float32)
        prod = jnp.ones((SUBLANES, hb), jnp.float32)
        for j in range(J):
            a = a_ref[_rows(j), cols]
            hloc = a * hloc + u_ref[_rows(j), cols]
            prod = a * prod
        hl_ref[:, cols] = hloc
        pl_ref[:, cols] = prod

        cur = jnp.where(seq_start, 0.0, hc_ref[0:1, cols])
        for s in range(SUBLANES):
            c_ref[s:s + 1, cols] = cur
            cur = hl_ref[s:s + 1, cols] + pl_ref[s:s + 1, cols] * cur
        hc_ref[0:1, cols] = cur
        hstate = c_ref[:, cols]

        for j2 in range(J // 2):
            merged = []
            for j in (2 * j2, 2 * j2 + 1):
                hstate = a_ref[_rows(j), cols] * hstate + u_ref[_rows(j), cols]
                y_rnn = jax.nn.gelu(z_ref[_rows(j), zcols(o_ry, cols)]) * hstate
                y_conv = z_ref[_rows(j), zcols(o_cb, cols)] * yc_ref[_rows(j), cols]
                merged.append(_sigmoid(z_ref[_rows(j), zcols(o_gr, cols)]) * y_rnn
                              + _sigmoid(z_ref[_rows(j), zcols(o_gc, cols)]) * y_conv)
            mrg_ref[_rows(2 * j2, 2), cols] = jnp.concatenate(merged, axis=0).astype(jnp.bfloat16)

    def output(h, cols):
        w_rows = slice(cols.start // 2, cols.stop // 2)
        return jnp.dot(mrg_ref[:, cols], _weights(wout_ref[w_rows, :]),
                       preferred_element_type=jnp.float32)

    starts = [(groups * g + k) % groups_per_seq == 0 for k in range(groups + 1)]
    slots = ((hn0_ref, z0_ref, xc0_ref, yc0_ref), (hn1_ref, z1_ref, xc1_ref, yc1_ref))
    xp_slots = (xp0_ref, xp1_ref)

    def group_rows(k):
        xs = x_ref[k * SUB:(k + 1) * SUB, :] if k < groups else xn_ref[...]
        if not permute_in:
            return xs
        xp_ref = xp_slots[k % 2]
        xp_ref[...] = _to_stream_order(xs)
        return xp_ref[...]

    @pl.when(g == 0)
    def _():
        p4_ref[...] = jnp.zeros_like(p4_ref)
        pq_ref[...] = jnp.zeros_like(pq_ref)
        hc_ref[...] = jnp.zeros_like(hc_ref)
        normalize(group_rows(0), hn0_ref)
        for chunk in range(len(chunks)):
            project(hn0_ref, z0_ref, chunk)
        for cols in heads:
            convs(z0_ref, xc0_ref, yc0_ref, cols, True)

    for k in range(groups):
        rows = slice(k * SUB, (k + 1) * SUB)
        _, z_cur, xc_cur, yc_cur = slots[k % 2]
        hn_nxt, z_nxt, xc_nxt, yc_nxt = slots[(k + 1) % 2]
        normalize(group_rows(k + 1), hn_nxt)
        acc = xp_slots[k % 2][...] if permute_in else x_ref[rows, :]
        for h, cols in enumerate(heads):
            gates(xc_cur, h, cols)
            project(hn_nxt, z_nxt, h)
            scan_merge(z_cur, yc_cur, cols, starts[k])
            if h > 0:
                acc = acc + output(h - 1, heads[h - 1])
        o_ref[rows, :] = acc + output(RG_HEADS - 1, heads[-1])
        for cols in heads:
            convs(z_nxt, xc_nxt, yc_nxt, cols, starts[k + 1])


def _ffn_kernel(x_ref, xn_ref, p_ref, gffn_ref, wgu_ref, wd_ref, gple_ref, wpg_ref,
                wple_ref, gfin_ref, o_ref, gu_ref, act0_ref, act1_ref, *,
                groups, d_ff, final):
    g = pl.program_id(0)

    def project(xs):
        hn = _rmsnorm(xs, gffn_ref[...]).astype(jnp.bfloat16)
        gu_ref[...] = jnp.dot(hn, _weights(wgu_ref[...]), preferred_element_type=jnp.float32)

    def activate(act_ref):
        gate = gu_ref[:, 0:d_ff]
        act_ref[...] = ((gate * _sigmoid(gate)) * gu_ref[:, d_ff:2 * d_ff]).astype(jnp.bfloat16)

    def down(act_ref, xs):
        return xs + jnp.dot(act_ref[...], _weights(wd_ref[...]), preferred_element_type=jnp.float32)

    def embed(x2, ps):
        hn2 = _rmsnorm(x2, gple_ref[...]).astype(jnp.bfloat16)
        gate = _sigmoid(jnp.dot(hn2, _weights(wpg_ref[...]), preferred_element_type=jnp.float32))
        pb = _to_stream_order(ps).astype(jnp.bfloat16)
        pe = jnp.dot(pb, _weights(wple_ref[...]), preferred_element_type=jnp.float32)
        x3 = x2 + gate * pe
        if final:
            x3 = _to_time_order(_rmsnorm(x3, gfin_ref[...]))
        return x3

    @pl.when(g == 0)
    def _():
        project(x_ref[0:FFN_SUB, :])
        activate(act0_ref)

    act_slots = (act0_ref, act1_ref)
    for k in range(groups):
        rows = slice(k * FFN_SUB, (k + 1) * FFN_SUB)
        nxt = slice((k + 1) * FFN_SUB, (k + 2) * FFN_SUB)
        x2 = down(act_slots[k % 2], x_ref[rows, :])
        project(x_ref[nxt, :] if k + 1 < groups else xn_ref[...])
        activate(act_slots[(k + 1) % 2])
        o_ref[rows, :] = embed(x2, p_ref[rows, :])


def _resident(shape):
    zeros = (0,) * len(shape)
    return pl.BlockSpec(shape, lambda g: zeros, pipeline_mode=pl.Buffered(1))


def _resident_layer(stacked_shape, layer):
    index = (layer,) + (0,) * (len(stacked_shape) - 1)
    return pl.BlockSpec((None,) + tuple(stacked_shape[1:]), lambda g: index,
                        pipeline_mode=pl.Buffered(1))


def _row_specs(n, d, sub, groups):
    last_group = n // sub - 1
    cur = pl.BlockSpec((groups * sub, d), lambda g: (g, 0))
    nxt = pl.BlockSpec((sub, d), lambda g: (jnp.minimum(groups * (g + 1), last_group), 0))
    return cur, nxt


def _mixer_call(x2d, gmix, win, c4w, c4b, wri, br, bi, lam, c3w, wout, *, layer, seq_len,
                permute_in):
    n, d = x2d.shape
    w_in_cols = win.shape[-1]
    f32 = jnp.float32
    groups = MIXER_GROUPS
    cur, nxt = _row_specs(n, d, SUB, groups)
    consts = (gmix, win, c4w, c4b, wri, br, bi, lam, c3w, wout)
    stacked = (win, wri, wout)
    return pl.pallas_call(
        functools.partial(_mixer_kernel, groups=groups, groups_per_seq=seq_len // SUB, d=d,
                          permute_in=permute_in),
        grid=(n // (groups * SUB),),
        in_specs=[cur, nxt] + [
            _resident_layer(c.shape, layer) if any(c is s for s in stacked)
            else _resident(c.shape) for c in consts],
        out_specs=cur,
        out_shape=jax.ShapeDtypeStruct((n, d), f32),
        scratch_shapes=[
            pltpu.VMEM((SUB, d), jnp.bfloat16),
            pltpu.VMEM((SUB, d), jnp.bfloat16),
            pltpu.VMEM((SUB, w_in_cols), f32),
            pltpu.VMEM((SUB, w_in_cols), f32),
            pltpu.VMEM((SUB, d), f32),
            pltpu.VMEM((SUB, d), f32),
            pltpu.VMEM((SUB, d), f32),
            pltpu.VMEM((SUB, d), f32),
            pltpu.VMEM((2, SUB, 2 * d), f32),
            pltpu.VMEM((SUB, d), f32),
            pltpu.VMEM((SUB, d), f32),
            pltpu.VMEM((SUBLANES, d), f32),
            pltpu.VMEM((SUBLANES, d), f32),
            pltpu.VMEM((SUB, d), jnp.bfloat16),
            pltpu.VMEM((_H4 * SUBLANES, d), f32),
            pltpu.VMEM((_H3 * SUBLANES, d), f32),
            pltpu.VMEM((SUBLANES, d), f32),
            pltpu.VMEM((SUBLANES, d), f32),
            pltpu.VMEM((SUB, d), f32),
            pltpu.VMEM((SUB, d), f32),
        ],
        compiler_params=pltpu.CompilerParams(
            dimension_semantics=("arbitrary",), vmem_limit_bytes=VMEM_LIMIT_BYTES),
        name="mixer",
    )(x2d, x2d, *consts)


def _ffn_call(x2d, p3d, gffn, wgu, wd, gple, wpg, wple, gfin, *, layer, final):
    n, d = x2d.shape
    d_ff = wgu.shape[-1] // 2
    groups, sub = FFN_GROUPS, FFN_SUB
    cur, nxt = _row_specs(n, d, sub, groups)
    p_spec = pl.BlockSpec((None, groups * sub, p3d.shape[-1]), lambda g: (layer, g, 0))
    consts = (gffn, wgu, wd, gple, wpg, wple, gfin)
    stacked = (wgu, wd, wpg, wple)
    return pl.pallas_call(
        functools.partial(_ffn_kernel, groups=groups, d_ff=d_ff, final=final),
        grid=(n // (groups * sub),),
        in_specs=[cur, nxt, p_spec] + [
            _resident_layer(c.shape, layer) if any(c is s for s in stacked)
            else _resident(c.shape) for c in consts],
        out_specs=cur,
        out_shape=jax.ShapeDtypeStruct((n, d), jnp.float32),
        scratch_shapes=[pltpu.VMEM((sub, 2 * d_ff), jnp.float32),
                        pltpu.VMEM((sub, d_ff), jnp.bfloat16),
                        pltpu.VMEM((sub, d_ff), jnp.bfloat16)],
        compiler_params=pltpu.CompilerParams(
            dimension_semantics=("arbitrary",), vmem_limit_bytes=VMEM_LIMIT_BYTES),
        name="ffn_ple",
    )(x2d, x2d, p3d, *consts)


def kernel(x, p, g_mix, w_in, conv4_w, conv4_b, w_rg_r, b_rg_r, w_rg_i, b_rg_i, lru_lambda, conv3_w, w_out, g_ffn, w_gate_up, w_down, g_ple, w_ple_gate, w_ple, g_final):
    bsz, seq_len, d = x.shape
    depth = p.shape[0]
    for groups, sub in ((MIXER_GROUPS, SUB), (FFN_GROUPS, FFN_SUB)):
        assert groups % 2 == 0 and seq_len % (groups * sub) == 0
    assert d % (RG_HEADS * LANES) == 0 and FFN_SUB == SUB
    row = lambda v: v.reshape(1, -1)
    rows8 = lambda v: jnp.repeat(v.reshape(-1, v.shape[-1]), SUBLANES, axis=0)

    def pack_heads(w):
        hb = w.shape[-1]
        return _pack_weights(w.reshape(-1, hb, hb)).reshape(depth, -1, hb // 2, hb)

    xs = x.reshape(bsz * seq_len, d)
    win_p, wout_p = _pack_weights(w_in), _pack_weights(w_out)
    wri_p = jnp.concatenate([pack_heads(w_rg_r), pack_heads(w_rg_i)], axis=-1)
    wgu_p, wd_p = _pack_weights(w_gate_up), _pack_weights(w_down)
    wpg_p, wple_p = _pack_weights(w_ple_gate), _pack_weights(w_ple)

    p3d = p.reshape(depth, bsz * seq_len, -1)

    for i in range(depth):
        xs = _mixer_call(
            xs, row(g_mix[i]), win_p, rows8(conv4_w[i]), rows8(conv4_b[i]),
            wri_p, rows8(b_rg_r[i].reshape(-1)), rows8(b_rg_i[i].reshape(-1)), rows8(lru_lambda[i]), rows8(conv3_w[i]),
            wout_p, layer=i, seq_len=seq_len, permute_in=(i == 0))
        xs = _ffn_call(
            xs, p3d, row(g_ffn[i]), wgu_p, wd_p,
            row(g_ple[i]), wpg_p, wple_p,
            row(g_final), layer=i, final=(i == depth - 1))
    return xs.reshape(bsz, seq_len, d)
```

```python
import functools
import math

import jax
import jax.numpy as jnp
from jax import lax
from jax.experimental import pallas as pl
from jax.experimental.pallas import tpu as pltpu

EPS = 1e-6
RG_C = 8.0
RG_HEADS = 4
CONV4_WIDTH = 4
CONV3_WIDTH = 3

SUBLANES = 8
LANES = 128
PACK_BLOCK_ROWS = 512
PACK_BLOCK_COLS = 2048
SUB = 128
J = SUB // SUBLANES
MIXER_GROUPS = 2
FFN_GROUPS = 4
FFN_SUB = 128
VMEM_LIMIT_BYTES = 60000 * 1024

_H3 = CONV3_WIDTH - 1
_H4 = CONV4_WIDTH - 1


def _to_stream_order(a):
    return jnp.swapaxes(a.reshape(SUBLANES, J, a.shape[-1]), 0, 1).reshape(a.shape)


def _to_time_order(a):
    return jnp.swapaxes(a.reshape(J, SUBLANES, a.shape[-1]), 0, 1).reshape(a.shape)


def _rmsnorm(xs, g):
    ms = jnp.mean(xs * xs, axis=-1, keepdims=True)
    return (xs * lax.rsqrt(ms + EPS)) * g


def _sigmoid(z):
    return 0.5 * jnp.tanh(0.5 * z) + 0.5


def _gelu_tanh(x):
    c0 = math.sqrt(2.0 / math.pi)
    half_x = 0.5 * x
    return half_x * jnp.tanh(x * (c0 + (0.044715 * c0) * (x * x))) + half_x


def _softplus(z):
    return jnp.maximum(z, 0.0) + jnp.log1p(jnp.exp(-jnp.abs(z)))


def _rows(j, n=1):
    return slice(j * SUBLANES, (j + n) * SUBLANES)


def _largest_block(size, unit, cap):
    return max(b for b in range(unit, min(size, cap) + 1, unit) if size % b == 0)


def _pack_kernel(w_ref, o_ref):
    o_ref[...] = pltpu.bitcast(w_ref[...].astype(jnp.bfloat16), jnp.uint32)


def _pack_weights(w):
    layers, k, n = w.shape
    bk = _largest_block(k, 2 * SUBLANES, PACK_BLOCK_ROWS)
    bn = _largest_block(n, LANES, PACK_BLOCK_COLS)
    return pl.pallas_call(
        _pack_kernel,
        grid=(layers, k // bk, n // bn),
        in_specs=[pl.BlockSpec((None, bk, bn), lambda l, i, j: (l, i, j))],
        out_specs=pl.BlockSpec((None, bk // 2, bn), lambda l, i, j: (l, i, j)),
        out_shape=jax.ShapeDtypeStruct((layers, k // 2, n), jnp.uint32),
        name="pack_weights",
    )(w)


def _weights(packed):
    return pltpu.bitcast(packed, jnp.bfloat16)


def _mixer_kernel(x_ref, xn_ref, gmix_ref, win_ref, c4w_ref, c4b_ref, wri_ref, br_ref,
                  bi_ref, lam_ref, c3w_ref, wout_ref, o_ref,
                  hn0_ref, hn1_ref, z0_ref, z1_ref, xc0_ref, xc1_ref, yc0_ref, yc1_ref,
                  pre_ref, a_ref, u_ref, hl_ref, pl_ref, mrg_ref,
                  p4_ref, pq_ref, hc_ref, c_ref, xp0_ref, xp1_ref, *,
                  groups, groups_per_seq, d, permute_in):
    g = pl.program_id(0)
    hb = d // RG_HEADS
    o_rx, o_ry, o_cb, o_cc, o_cx, o_gr, o_gc = (k * d for k in range(7))
    first_stream = lax.broadcasted_iota(jnp.int32, (SUBLANES, hb), 0) == 0
    heads = [slice(h * hb, (h + 1) * hb) for h in range(RG_HEADS)]
    chunks = [(o_rx, o_ry), (o_cc, o_gr), (o_ry, o_cc), (o_gr, o_gc + d)]

    def zcols(off, cols):
        return slice(off + cols.start, off + cols.stop)

    def from_prev_stream(cur, prev):
        return jnp.where(first_stream, pltpu.roll(prev, 1, 0), pltpu.roll(cur, 1, 0))

    def normalize(xs, hn_ref):
        hn_ref[...] = _rmsnorm(xs, gmix_ref[...]).astype(jnp.bfloat16)

    def project(hn_ref, z_ref, chunk):
        lo, hi = chunks[chunk]
        z_ref[:, lo:hi] = jnp.dot(hn_ref[...], _weights(win_ref[:, lo:hi]),
                                  preferred_element_type=jnp.float32)

    def convs(z_ref, xc_ref, yc_ref, cols, seq_start):
        def history(h_ref, jj):
            return jnp.where(seq_start, 0.0, h_ref[_rows(jj), cols])

        c_rx = zcols(o_rx, cols)
        halo4 = [from_prev_stream(z_ref[_rows(J - _H4 + jj), c_rx], history(p4_ref, jj))
                 for jj in range(_H4)]
        p4_ref[:, cols] = z_ref[_rows(J - _H4, _H4), c_rx]

        def rx(j):
            return halo4[j + _H4] if j < 0 else z_ref[_rows(j), c_rx]

        for j in range(J):
            acc = c4b_ref[:, cols] + c4w_ref[_rows(0), cols] * rx(j - _H4)
            for k in range(1, CONV4_WIDTH):
                acc = acc + c4w_ref[_rows(k), cols] * rx(j - _H4 + k)
            xc_ref[_rows(j), cols] = acc

        def q(j):
            return z_ref[_rows(j), zcols(o_cc, cols)] * z_ref[_rows(j), zcols(o_cx, cols)]

        tail = [q(J - _H3 + jj) for jj in range(_H3)]
        window = [from_prev_stream(tail[jj], history(pq_ref, jj)) for jj in range(_H3)]
        for jj in range(_H3):
            pq_ref[_rows(jj), cols] = tail[jj]
        for j in range(J):
            qj = q(j) if j < J - _H3 else tail[j - (J - _H3)]
            yc = c3w_ref[_rows(_H3), cols] * qj
            for k in range(_H3):
                yc = yc + c3w_ref[_rows(k), cols] * window[k]
            yc_ref[_rows(j), cols] = yc
            window = window[1:] + [qj]

    def gates(xc_ref, h, cols):
        xh = xc_ref[:, cols]
        xb = xh.astype(jnp.bfloat16)
        slot = lax.rem(g, 2)
        r_cols = slice(2 * cols.start, 2 * cols.start + hb)
        i_cols = slice(2 * cols.start + hb, 2 * cols.stop)
        pre_ref[slot, :, 2 * cols.start:2 * cols.stop] = jnp.dot(
            xb, _weights(wri_ref[h]), preferred_element_type=jnp.float32)
        half_scale = (-0.5 * RG_C) * _softplus(-lam_ref[:, cols])
        hloc = jnp.zeros((SUBLANES, hb), jnp.float32)
        prod = jnp.ones((SUBLANES, hb), jnp.float32)
        for j in range(J):
            t_r = jnp.tanh(0.5 * (pre_ref[slot, _rows(j), r_cols] + br_ref[:, cols]))
            gi = _sigmoid(pre_ref[slot, _rows(j), i_cols] + bi_ref[:, cols])
            log_a = half_scale * t_r + half_scale
            a = jnp.exp(log_a)
            w = -jnp.tanh(log_a) * (1.0 + a * a)
            mult = jnp.where(w > 0.0, w * lax.rsqrt(w), 0.0)
            u = mult * (gi * xh[_rows(j), :])
            a_ref[_rows(j), cols] = a
            u_ref[_rows(j), cols] = u
            hloc = a * hloc + u
            prod = a * prod
        hl_ref[:, cols] = hloc
        pl_ref[:, cols] = prod

    def scan_merge(z_ref, yc_ref, cols, seq_start):
        cur = jnp.where(seq_start, 0.0, hc_ref[0:1, cols])
        for s in range(SUBLANES):
            c_ref[s:s + 1, cols] = cur
            cur = hl_ref[s:s + 1, cols] + pl_ref[s:s + 1, cols] * cur
        hc_ref[0:1, cols] = cur
        hstate = c_ref[:, cols]

        for j2 in range(J // 2):
            merged = []
            for j in (2 * j2, 2 * j2 + 1):
                hstate = a_ref[_rows(j), cols] * hstate + u_ref[_rows(j), cols]
                y_rnn = _gelu_tanh(z_ref[_rows(j), zcols(o_ry, cols)]) * hstate
                y_conv = z_ref[_rows(j), zcols(o_cb, cols)] * yc_ref[_rows(j), cols]
                t_rnn = jnp.tanh(0.5 * z_ref[_rows(j), zcols(o_gr, cols)])
                t_conv = jnp.tanh(0.5 * z_ref[_rows(j), zcols(o_gc, cols)])
                merged.append(0.5 * ((t_rnn + 1.0) * y_rnn + (t_conv + 1.0) * y_conv))
            mrg_ref[_rows(2 * j2, 2), cols] = jnp.concatenate(merged, axis=0).astype(jnp.bfloat16)

    def output(h, cols):
        w_rows = slice(cols.start // 2, cols.stop // 2)
        return jnp.dot(mrg_ref[:, cols], _weights(wout_ref[w_rows, :]),
                       preferred_element_type=jnp.float32)

    starts = [(groups * g + k) % groups_per_seq == 0 for k in range(groups + 1)]
    slots = ((hn0_ref, z0_ref, xc0_ref, yc0_ref), (hn1_ref, z1_ref, xc1_ref, yc1_ref))
    xp_slots = (xp0_ref, xp1_ref)

    def group_rows(k):
        xs = x_ref[k * SUB:(k + 1) * SUB, :] if k < groups else xn_ref[...]
        if not permute_in:
            return xs
        xp_ref = xp_slots[k % 2]
        xp_ref[...] = _to_stream_order(xs)
        return xp_ref[...]

    @pl.when(g == 0)
    def _():
        p4_ref[...] = jnp.zeros_like(p4_ref)
        pq_ref[...] = jnp.zeros_like(pq_ref)
        hc_ref[...] = jnp.zeros_like(hc_ref)
        normalize(group_rows(0), hn0_ref)
        for chunk in range(len(chunks)):
            project(hn0_ref, z0_ref, chunk)
        for cols in heads:
            convs(z0_ref, xc0_ref, yc0_ref, cols, True)

    for k in range(groups):
        rows = slice(k * SUB, (k + 1) * SUB)
        _, z_cur, xc_cur, yc_cur = slots[k % 2]
        hn_nxt, z_nxt, xc_nxt, yc_nxt = slots[(k + 1) % 2]
        normalize(group_rows(k + 1), hn_nxt)
        acc = xp_slots[k % 2][...] if permute_in else x_ref[rows, :]
        for h, cols in enumerate(heads):
            gates(xc_cur, h, cols)
            project(hn_nxt, z_nxt, h)
            scan_merge(z_cur, yc_cur, cols, starts[k])
            if h > 0:
                acc = acc + output(h - 1, heads[h - 1])
        o_ref[rows, :] = acc + output(RG_HEADS - 1, heads[-1])
        for cols in heads:
            convs(z_nxt, xc_nxt, yc_nxt, cols, starts[k + 1])


def _ffn_kernel(x_ref, xn_ref, p_ref, gffn_ref, wgu_ref, wd_ref, gple_ref, wpg_ref,
                wple_ref, gfin_ref, o_ref, gu_ref, act0_ref, act1_ref, *,
                groups, d_ff, final):
    g = pl.program_id(0)

    def project(xs):
        hn = _rmsnorm(xs, gffn_ref[...]).astype(jnp.bfloat16)
        gu_ref[...] = jnp.dot(hn, _weights(wgu_ref[...]), preferred_element_type=jnp.float32)

    def activate(act_ref):
        gate = gu_ref[:, 0:d_ff]
        act_ref[...] = ((gate * _sigmoid(gate)) * gu_ref[:, d_ff:2 * d_ff]).astype(jnp.bfloat16)

    def down(act_ref, xs):
        return xs + jnp.dot(act_ref[...], _weights(wd_ref[...]), preferred_element_type=jnp.float32)

    def embed(x2, ps):
        hn2 = _rmsnorm(x2, gple_ref[...]).astype(jnp.bfloat16)
        gate = _sigmoid(jnp.dot(hn2, _weights(wpg_ref[...]), preferred_element_type=jnp.float32))
        pb = _to_stream_order(ps).astype(jnp.bfloat16)
        pe = jnp.dot(pb, _weights(wple_ref[...]), preferred_element_type=jnp.float32)
        x3 = x2 + gate * pe
        if final:
            x3 = _to_time_order(_rmsnorm(x3, gfin_ref[...]))
        return x3

    @pl.when(g == 0)
    def _():
        project(x_ref[0:FFN_SUB, :])
        activate(act0_ref)

    act_slots = (act0_ref, act1_ref)
    for k in range(groups):
        rows = slice(k * FFN_SUB, (k + 1) * FFN_SUB)
        nxt = slice((k + 1) * FFN_SUB, (k + 2) * FFN_SUB)
        x2 = down(act_slots[k % 2], x_ref[rows, :])
        project(x_ref[nxt, :] if k + 1 < groups else xn_ref[...])
        activate(act_slots[(k + 1) % 2])
        o_ref[rows, :] = embed(x2, p_ref[rows, :])


def _resident(shape):
    zeros = (0,) * len(shape)
    return pl.BlockSpec(shape, lambda g: zeros, pipeline_mode=pl.Buffered(1))


def _resident_layer(stacked_shape, layer):
    index = (layer,) + (0,) * (len(stacked_shape) - 1)
    return pl.BlockSpec((None,) + tuple(stacked_shape[1:]), lambda g: index,
                        pipeline_mode=pl.Buffered(1))


def _row_specs(n, d, sub, groups):
    last_group = n // sub - 1
    cur = pl.BlockSpec((groups * sub, d), lambda g: (g, 0))
    nxt = pl.BlockSpec((sub, d), lambda g: (jnp.minimum(groups * (g + 1), last_group), 0))
    return cur, nxt


def _mixer_call(x2d, gmix, win, c4w, c4b, wri, br, bi, lam, c3w, wout, *, layer, seq_len,
                permute_in):
    n, d = x2d.shape
    w_in_cols = win.shape[-1]
    f32 = jnp.float32
    groups = MIXER_GROUPS
    cur, nxt = _row_specs(n, d, SUB, groups)
    consts = (gmix, win, c4w, c4b, wri, br, bi, lam, c3w, wout)
    stacked = (win, wri, wout)
    return pl.pallas_call(
        functools.partial(_mixer_kernel, groups=groups, groups_per_seq=seq_len // SUB, d=d,
                          permute_in=permute_in),
        grid=(n // (groups * SUB),),
        in_specs=[cur, nxt] + [
            _resident_layer(c.shape, layer) if any(c is s for s in stacked)
            else _resident(c.shape) for c in consts],
        out_specs=cur,
        out_shape=jax.ShapeDtypeStruct((n, d), f32),
        scratch_shapes=[
            pltpu.VMEM((SUB, d), jnp.bfloat16),
            pltpu.VMEM((SUB, d), jnp.bfloat16),
            pltpu.VMEM((SUB, w_in_cols), f32),
            pltpu.VMEM((SUB, w_in_cols), f32),
            pltpu.VMEM((SUB, d), f32),
            pltpu.VMEM((SUB, d), f32),
            pltpu.VMEM((SUB, d), f32),
            pltpu.VMEM((SUB, d), f32),
            pltpu.VMEM((2, SUB, 2 * d), f32),
            pltpu.VMEM((SUB, d), f32),
            pltpu.VMEM((SUB, d), f32),
            pltpu.VMEM((SUBLANES, d), f32),
            pltpu.VMEM((SUBLANES, d), f32),
            pltpu.VMEM((SUB, d), jnp.bfloat16),
            pltpu.VMEM((_H4 * SUBLANES, d), f32),
            pltpu.VMEM((_H3 * SUBLANES, d), f32),
            pltpu.VMEM((SUBLANES, d), f32),
            pltpu.VMEM((SUBLANES, d), f32),
            pltpu.VMEM((SUB, d), f32),
            pltpu.VMEM((SUB, d), f32),
        ],
        compiler_params=pltpu.CompilerParams(
            dimension_semantics=("arbitrary",), vmem_limit_bytes=VMEM_LIMIT_BYTES),
        name="mixer",
    )(x2d, x2d, *consts)


def _ffn_call(x2d, p3d, gffn, wgu, wd, gple, wpg, wple, gfin, *, layer, final):
    n, d = x2d.shape
    d_ff = wgu.shape[-1] // 2
    groups, sub = FFN_GROUPS, FFN_SUB
    cur, nxt = _row_specs(n, d, sub, groups)
    p_spec = pl.BlockSpec((None, groups * sub, p3d.shape[-1]), lambda g: (layer, g, 0))
    consts = (gffn, wgu, wd, gple, wpg, wple, gfin)
    stacked = (wgu, wd, wpg, wple)
    return pl.pallas_call(
        functools.partial(_ffn_kernel, groups=groups, d_ff=d_ff, final=final),
        grid=(n // (groups * sub),),
        in_specs=[cur, nxt, p_spec] + [
            _resident_layer(c.shape, layer) if any(c is s for s in stacked)
            else _resident(c.shape) for c in consts],
        out_specs=cur,
        out_shape=jax.ShapeDtypeStruct((n, d), jnp.float32),
        scratch_shapes=[pltpu.VMEM((sub, 2 * d_ff), jnp.float32),
                        pltpu.VMEM((sub, d_ff), jnp.bfloat16),
                        pltpu.VMEM((sub, d_ff), jnp.bfloat16)],
        compiler_params=pltpu.CompilerParams(
            dimension_semantics=("arbitrary",), vmem_limit_bytes=VMEM_LIMIT_BYTES),
        name="ffn_ple",
    )(x2d, x2d, p3d, *consts)


def kernel(x, p, g_mix, w_in, conv4_w, conv4_b, w_rg_r, b_rg_r, w_rg_i, b_rg_i, lru_lambda, conv3_w, w_out, g_ffn, w_gate_up, w_down, g_ple, w_ple_gate, w_ple, g_final):
    bsz, seq_len, d = x.shape
    depth = p.shape[0]
    for groups, sub in ((MIXER_GROUPS, SUB), (FFN_GROUPS, FFN_SUB)):
        assert groups % 2 == 0 and seq_len % (groups * sub) == 0
    assert d % (RG_HEADS * LANES) == 0 and FFN_SUB == SUB
    row = lambda v: v.reshape(1, -1)
    rows8 = lambda v: jnp.repeat(v.reshape(-1, v.shape[-1]), SUBLANES, axis=0)

    def pack_heads(w):
        hb = w.shape[-1]
        return _pack_weights(w.reshape(-1, hb, hb)).reshape(depth, -1, hb // 2, hb)

    xs = x.reshape(bsz * seq_len, d)
    win_p, wout_p = _pack_weights(w_in), _pack_weights(w_out)
    wri_p = jnp.concatenate([pack_heads(w_rg_r), pack_heads(w_rg_i)], axis=-1)
    wgu_p, wd_p = _pack_weights(w_gate_up), _pack_weights(w_down)
    wpg_p, wple_p = _pack_weights(w_ple_gate), _pack_weights(w_ple)

    p3d = p.reshape(depth, bsz * seq_len, -1)

    for i in range(depth):
        xs = _mixer_call(
            xs, row(g_mix[i]), win_p, rows8(conv4_w[i]), rows8(conv4_b[i]),
            wri_p, rows8(b_rg_r[i].reshape(-1)), rows8(b_rg_i[i].reshape(-1)), rows8(lru_lambda[i]), rows8(conv3_w[i]),
            wout_p, layer=i, seq_len=seq_len, permute_in=(i == 0))
        xs = _ffn_call(
            xs, p3d, row(g_ffn[i]), wgu_p, wd_p,
            row(g_ple[i]), wpg_p, wple_p,
            row(g_final), layer=i, final=(i == depth - 1))
    return xs.reshape(bsz, seq_len, d)
```

```python
import functools

import jax
import jax.numpy as jnp
from jax import lax
from jax.experimental import pallas as pl
from jax.experimental.pallas import tpu as pltpu

EPS = 1e-6
RG_C = 8.0
RG_HEADS = 4
CONV4_WIDTH = 4
CONV3_WIDTH = 3

SUBLANES = 8
LANES = 128
PACK_BLOCK_ROWS = 512
PACK_BLOCK_COLS = 2048
SUB = 128
J = SUB // SUBLANES
MIXER_GROUPS = 2
FFN_GROUPS = 4
FFN_SUB = 128
GATE_UP_BLOCK = 256
VMEM_LIMIT_BYTES = 60000 * 1024

_H3 = CONV3_WIDTH - 1
_H4 = CONV4_WIDTH - 1


def _to_stream_order(a):
    return jnp.swapaxes(a.reshape(SUBLANES, J, a.shape[-1]), 0, 1).reshape(a.shape)


def _to_time_order(a):
    return jnp.swapaxes(a.reshape(J, SUBLANES, a.shape[-1]), 0, 1).reshape(a.shape)


def _rmsnorm(xs, g):
    ms = jnp.mean(xs * xs, axis=-1, keepdims=True)
    return (xs * lax.rsqrt(ms + EPS)) * g


def _sigmoid(z):
    return 0.5 * jnp.tanh(0.5 * z) + 0.5


def _softplus(z):
    return jnp.maximum(z, 0.0) + jnp.log1p(jnp.exp(-jnp.abs(z)))


def _rows(j, n=1):
    return slice(j * SUBLANES, (j + n) * SUBLANES)


def _largest_block(size, unit, cap):
    return max(b for b in range(unit, min(size, cap) + 1, unit) if size % b == 0)


def _pack_kernel(w_ref, o_ref):
    o_ref[...] = pltpu.bitcast(w_ref[...].astype(jnp.bfloat16), jnp.uint32)


def _pack_weights(w, interleave_halves=None):
    layers, k, n = w.shape
    if interleave_halves is None:
        bk = _largest_block(k, 2 * SUBLANES, PACK_BLOCK_ROWS)
        bn = _largest_block(n, LANES, PACK_BLOCK_COLS)
        source = lambda j: j
    else:
        bk, bn = k, interleave_halves
        per_half = n // (2 * bn)
        source = lambda j: (j % 2) * per_half + j // 2
    return pl.pallas_call(
        _pack_kernel,
        grid=(layers, k // bk, n // bn),
        in_specs=[pl.BlockSpec((None, bk, bn), lambda l, i, j: (l, i, source(j)))],
        out_specs=pl.BlockSpec((None, bk // 2, bn), lambda l, i, j: (l, i, j)),
        out_shape=jax.ShapeDtypeStruct((layers, k // 2, n), jnp.uint32),
        name="pack_weights",
    )(w)


def _weights(packed):
    return pltpu.bitcast(packed, jnp.bfloat16)


def _mixer_kernel(x_ref, xn_ref, gmix_ref, win_ref, c4w_ref, c4b_ref, wri_ref, br_ref,
                  bi_ref, lam_ref, c3w_ref, wout_ref, o_ref,
                  hn0_ref, hn1_ref, z0_ref, z1_ref, xc0_ref, xc1_ref, yc0_ref, yc1_ref,
                  pre_ref, a_ref, u_ref, hl_ref, pl_ref, mrg_ref,
                  p4_ref, pq_ref, hc_ref, c_ref, xp0_ref, xp1_ref, *,
                  groups, groups_per_seq, d, permute_in):
    g = pl.program_id(0)
    hb = d // RG_HEADS
    o_rx, o_ry, o_cb, o_cc, o_cx, o_gr, o_gc = (k * d for k in range(7))
    first_stream = lax.broadcasted_iota(jnp.int32, (SUBLANES, hb), 0) == 0
    heads = [slice(h * hb, (h + 1) * hb) for h in range(RG_HEADS)]
    chunks = [(o_rx, o_ry), (o_cc, o_gr), (o_ry, o_cc), (o_gr, o_gc + d)]

    def zcols(off, cols):
        return slice(off + cols.start, off + cols.stop)

    def from_prev_stream(cur, prev):
        return jnp.where(first_stream, pltpu.roll(prev, 1, 0), pltpu.roll(cur, 1, 0))

    def normalize(xs, hn_ref):
        hn_ref[...] = _rmsnorm(xs, gmix_ref[...]).astype(jnp.bfloat16)

    def project(hn_ref, z_ref, chunk):
        lo, hi = chunks[chunk]
        z_ref[:, lo:hi] = jnp.dot(hn_ref[...], _weights(win_ref[:, lo:hi]),
                                  preferred_element_type=jnp.float32)

    def convs(z_ref, xc_ref, yc_ref, cols, seq_start):
        def history(h_ref, jj):
            return jnp.where(seq_start, 0.0, h_ref[_rows(jj), cols])

        c_rx = zcols(o_rx, cols)
        halo4 = [from_prev_stream(z_ref[_rows(J - _H4 + jj), c_rx], history(p4_ref, jj))
                 for jj in range(_H4)]
        p4_ref[:, cols] = z_ref[_rows(J - _H4, _H4), c_rx]

        def rx(j):
            return halo4[j + _H4] if j < 0 else z_ref[_rows(j), c_rx]

        for j in range(J):
            acc = c4b_ref[:, cols] + c4w_ref[_rows(0), cols] * rx(j - _H4)
            for k in range(1, CONV4_WIDTH):
                acc = acc + c4w_ref[_rows(k), cols] * rx(j - _H4 + k)
            xc_ref[_rows(j), cols] = acc

        def q(j):
            return z_ref[_rows(j), zcols(o_cc, cols)] * z_ref[_rows(j), zcols(o_cx, cols)]

        tail = [q(J - _H3 + jj) for jj in range(_H3)]
        window = [from_prev_stream(tail[jj], history(pq_ref, jj)) for jj in range(_H3)]
        for jj in range(_H3):
            pq_ref[_rows(jj), cols] = tail[jj]
        for j in range(J):
            qj = q(j) if j < J - _H3 else tail[j - (J - _H3)]
            yc = c3w_ref[_rows(_H3), cols] * qj
            for k in range(_H3):
                yc = yc + c3w_ref[_rows(k), cols] * window[k]
            yc_ref[_rows(j), cols] = yc
            window = window[1:] + [qj]

    def gates(xc_ref, h, cols):
        xh = xc_ref[:, cols]
        xb = xh.astype(jnp.bfloat16)
        slot = lax.rem(g, 2)
        r_cols = slice(2 * cols.start, 2 * cols.start + hb)
        i_cols = slice(2 * cols.start + hb, 2 * cols.stop)
        pre_ref[slot, :, 2 * cols.start:2 * cols.stop] = jnp.dot(
            xb, _weights(wri_ref[h]), preferred_element_type=jnp.float32)
        half_scale = (-0.5 * RG_C) * _softplus(-lam_ref[:, cols])
        for j in range(J):
            t_r = jnp.tanh(0.5 * (pre_ref[slot, _rows(j), r_cols] + br_ref[:, cols]))
            gi = _sigmoid(pre_ref[slot, _rows(j), i_cols] + bi_ref[:, cols])
            log_a = half_scale * t_r + half_scale
            a = jnp.exp(log_a)
            w = -jnp.tanh(log_a) * (1.0 + a * a)
            mult = jnp.where(w > 0.0, w * lax.rsqrt(w), 0.0)
            a_ref[_rows(j), cols] = a
            u_ref[_rows(j), cols] = mult * (gi * xh[_rows(j), :])

    def scan_merge(z_ref, yc_ref, cols, seq_start):
        hloc = jnp.zeros((SUBLANES, hb), jnp.float32)
        prod = jnp.ones((SUBLANES, hb), jnp.float32)
        for j in range(J):
            a = a_ref[_rows(j), cols]
            hloc = a * hloc + u_ref[_rows(j), cols]
            prod = a * prod
        hl_ref[:, cols] = hloc
        pl_ref[:, cols] = prod

        cur = jnp.where(seq_start, 0.0, hc_ref[0:1, cols])
        for s in range(SUBLANES):
            c_ref[s:s + 1, cols] = cur
            cur = hl_ref[s:s + 1, cols] + pl_ref[s:s + 1, cols] * cur
        hc_ref[0:1, cols] = cur
        hstate = c_ref[:, cols]

        for j2 in range(J // 2):
            merged = []
            for j in (2 * j2, 2 * j2 + 1):
                hstate = a_ref[_rows(j), cols] * hstate + u_ref[_rows(j), cols]
                y_rnn = jax.nn.gelu(z_ref[_rows(j), zcols(o_ry, cols)]) * hstate
                y_conv = z_ref[_rows(j), zcols(o_cb, cols)] * yc_ref[_rows(j), cols]
                merged.append(_sigmoid(z_ref[_rows(j), zcols(o_gr, cols)]) * y_rnn
                              + _sigmoid(z_ref[_rows(j), zcols(o_gc, cols)]) * y_conv)
            mrg_ref[_rows(2 * j2, 2), cols] = jnp.concatenate(merged, axis=0).astype(jnp.bfloat16)

    def output(h, cols):
        w_rows = slice(cols.start // 2, cols.stop // 2)
        return jnp.dot(mrg_ref[:, cols], _weights(wout_ref[w_rows, :]),
                       preferred_element_type=jnp.float32)

    starts = [(groups * g + k) % groups_per_seq == 0 for k in range(groups + 1)]
    slots = ((hn0_ref, z0_ref, xc0_ref, yc0_ref), (hn1_ref, z1_ref, xc1_ref, yc1_ref))
    xp_slots = (xp0_ref, xp1_ref)

    def group_rows(k):
        xs = x_ref[k * SUB:(k + 1) * SUB, :] if k < groups else xn_ref[...]
        if not permute_in:
            return xs
        xp_ref = xp_slots[k % 2]
        xp_ref[...] = _to_stream_order(xs)
        return xp_ref[...]

    @pl.when(g == 0)
    def _():
        p4_ref[...] = jnp.zeros_like(p4_ref)
        pq_ref[...] = jnp.zeros_like(pq_ref)
        hc_ref[...] = jnp.zeros_like(hc_ref)
        normalize(group_rows(0), hn0_ref)
        for chunk in range(len(chunks)):
            project(hn0_ref, z0_ref, chunk)
        for cols in heads:
            convs(z0_ref, xc0_ref, yc0_ref, cols, True)

    for k in range(groups):
        rows = slice(k * SUB, (k + 1) * SUB)
        _, z_cur, xc_cur, yc_cur = slots[k % 2]
        hn_nxt, z_nxt, xc_nxt, yc_nxt = slots[(k + 1) % 2]
        normalize(group_rows(k + 1), hn_nxt)
        acc = xp_slots[k % 2][...] if permute_in else x_ref[rows, :]
        for h, cols in enumerate(heads):
            gates(xc_cur, h, cols)
            project(hn_nxt, z_nxt, h)
            scan_merge(z_cur, yc_cur, cols, starts[k])
            if h > 0:
                acc = acc + output(h - 1, heads[h - 1])
        o_ref[rows, :] = acc + output(RG_HEADS - 1, heads[-1])
        for cols in heads:
            convs(z_nxt, xc_nxt, yc_nxt, cols, starts[k + 1])


def _ffn_kernel(x_ref, xn_ref, p_ref, gffn_ref, wgu_ref, wd_ref, gple_ref, wpg_ref,
                wple_ref, gfin_ref, o_ref, act0_ref, act1_ref, *,
                groups, d_ff, final):
    g = pl.program_id(0)

    def project(xs, act_ref):
        hn = _rmsnorm(xs, gffn_ref[...]).astype(jnp.bfloat16)
        gu = jnp.dot(hn, _weights(wgu_ref[...]), preferred_element_type=jnp.float32)
        b = GATE_UP_BLOCK
        for t in range(d_ff // b):
            gate = gu[:, 2 * t * b:(2 * t + 1) * b]
            up = gu[:, (2 * t + 1) * b:(2 * t + 2) * b]
            act_ref[:, t * b:(t + 1) * b] = ((gate * _sigmoid(gate)) * up).astype(jnp.bfloat16)

    def down(act_ref, xs):
        return xs + jnp.dot(act_ref[...], _weights(wd_ref[...]), preferred_element_type=jnp.float32)

    def embed(x2, ps):
        hn2 = _rmsnorm(x2, gple_ref[...]).astype(jnp.bfloat16)
        gate = _sigmoid(jnp.dot(hn2, _weights(wpg_ref[...]), preferred_element_type=jnp.float32))
        pb = _to_stream_order(ps).astype(jnp.bfloat16)
        pe = jnp.dot(pb, _weights(wple_ref[...]), preferred_element_type=jnp.float32)
        x3 = x2 + gate * pe
        if final:
            x3 = _to_time_order(_rmsnorm(x3, gfin_ref[...]))
        return x3

    @pl.when(g == 0)
    def _():
        project(x_ref[0:FFN_SUB, :], act0_ref)

    act_slots = (act0_ref, act1_ref)
    for k in range(groups):
        rows = slice(k * FFN_SUB, (k + 1) * FFN_SUB)
        nxt = slice((k + 1) * FFN_SUB, (k + 2) * FFN_SUB)
        x2 = down(act_slots[k % 2], x_ref[rows, :])
        project(x_ref[nxt, :] if k + 1 < groups else xn_ref[...], act_slots[(k + 1) % 2])
        o_ref[rows, :] = embed(x2, p_ref[rows, :])


def _resident(shape):
    zeros = (0,) * len(shape)
    return pl.BlockSpec(shape, lambda g: zeros, pipeline_mode=pl.Buffered(1))


def _resident_layer(stacked_shape, layer):
    index = (layer,) + (0,) * (len(stacked_shape) - 1)
    return pl.BlockSpec((None,) + tuple(stacked_shape[1:]), lambda g: index,
                        pipeline_mode=pl.Buffered(1))


def _row_specs(n, d, sub, groups):
    last_group = n // sub - 1
    cur = pl.BlockSpec((groups * sub, d), lambda g: (g, 0))
    nxt = pl.BlockSpec((sub, d), lambda g: (jnp.minimum(groups * (g + 1), last_group), 0))
    return cur, nxt


def _mixer_call(x2d, gmix, win, c4w, c4b, wri, br, bi, lam, c3w, wout, *, layer, seq_len,
                permute_in):
    n, d = x2d.shape
    w_in_cols = win.shape[-1]
    f32 = jnp.float32
    groups = MIXER_GROUPS
    cur, nxt = _row_specs(n, d, SUB, groups)
    consts = (gmix, win, c4w, c4b, wri, br, bi, lam, c3w, wout)
    stacked = (win, wri, wout)
    return pl.pallas_call(
        functools.partial(_mixer_kernel, groups=groups, groups_per_seq=seq_len // SUB, d=d,
                          permute_in=permute_in),
        grid=(n // (groups * SUB),),
        in_specs=[cur, nxt] + [
            _resident_layer(c.shape, layer) if any(c is s for s in stacked)
            else _resident(c.shape) for c in consts],
        out_specs=cur,
        out_shape=jax.ShapeDtypeStruct((n, d), f32),
        scratch_shapes=[
            pltpu.VMEM((SUB, d), jnp.bfloat16),
            pltpu.VMEM((SUB, d), jnp.bfloat16),
            pltpu.VMEM((SUB, w_in_cols), f32),
            pltpu.VMEM((SUB, w_in_cols), f32),
            pltpu.VMEM((SUB, d), f32),
            pltpu.VMEM((SUB, d), f32),
            pltpu.VMEM((SUB, d), f32),
            pltpu.VMEM((SUB, d), f32),
            pltpu.VMEM((2, SUB, 2 * d), f32),
            pltpu.VMEM((SUB, d), f32),
            pltpu.VMEM((SUB, d), f32),
            pltpu.VMEM((SUBLANES, d), f32),
            pltpu.VMEM((SUBLANES, d), f32),
            pltpu.VMEM((SUB, d), jnp.bfloat16),
            pltpu.VMEM((_H4 * SUBLANES, d), f32),
            pltpu.VMEM((_H3 * SUBLANES, d), f32),
            pltpu.VMEM((SUBLANES, d), f32),
            pltpu.VMEM((SUBLANES, d), f32),
            pltpu.VMEM((SUB, d), f32),
            pltpu.VMEM((SUB, d), f32),
        ],
        compiler_params=pltpu.CompilerParams(
            dimension_semantics=("arbitrary",), vmem_limit_bytes=VMEM_LIMIT_BYTES),
        name="mixer",
    )(x2d, x2d, *consts)


def _ffn_call(x2d, p3d, gffn, wgu, wd, gple, wpg, wple, gfin, *, layer, final):
    n, d = x2d.shape
    d_ff = wgu.shape[-1] // 2
    groups, sub = FFN_GROUPS, FFN_SUB
    cur, nxt = _row_specs(n, d, sub, groups)
    p_spec = pl.BlockSpec((None, groups * sub, p3d.shape[-1]), lambda g: (layer, g, 0))
    consts = (gffn, wgu, wd, gple, wpg, wple, gfin)
    stacked = (wgu, wd, wpg, wple)
    return pl.pallas_call(
        functools.partial(_ffn_kernel, groups=groups, d_ff=d_ff, final=final),
        grid=(n // (groups * sub),),
        in_specs=[cur, nxt, p_spec] + [
            _resident_layer(c.shape, layer) if any(c is s for s in stacked)
            else _resident(c.shape) for c in consts],
        out_specs=cur,
        out_shape=jax.ShapeDtypeStruct((n, d), jnp.float32),
        scratch_shapes=[pltpu.VMEM((sub, d_ff), jnp.bfloat16),
                        pltpu.VMEM((sub, d_ff), jnp.bfloat16)],
        compiler_params=pltpu.CompilerParams(
            dimension_semantics=("arbitrary",), vmem_limit_bytes=VMEM_LIMIT_BYTES),
        name="ffn_ple",
    )(x2d, x2d, p3d, *consts)


def kernel(x, p, g_mix, w_in, conv4_w, conv4_b, w_rg_r, b_rg_r, w_rg_i, b_rg_i, lru_lambda, conv3_w, w_out, g_ffn, w_gate_up, w_down, g_ple, w_ple_gate, w_ple, g_final):
    bsz, seq_len, d = x.shape
    depth = p.shape[0]
    for groups, sub in ((MIXER_GROUPS, SUB), (FFN_GROUPS, FFN_SUB)):
        assert groups % 2 == 0 and seq_len % (groups * sub) == 0
    assert d % (RG_HEADS * LANES) == 0 and FFN_SUB == SUB
    row = lambda v: v.reshape(1, -1)
    rows8 = lambda v: jnp.repeat(v.reshape(-1, v.shape[-1]), SUBLANES, axis=0)

    def pack_heads(w):
        hb = w.shape[-1]
        return _pack_weights(w.reshape(-1, hb, hb)).reshape(depth, -1, hb // 2, hb)

    xs = x.reshape(bsz * seq_len, d)
    win_p, wout_p = _pack_weights(w_in), _pack_weights(w_out)
    wri_p = jnp.concatenate([pack_heads(w_rg_r), pack_heads(w_rg_i)], axis=-1)
    wgu_p = _pack_weights(w_gate_up, interleave_halves=GATE_UP_BLOCK)
    wd_p = _pack_weights(w_down)
    wpg_p, wple_p = _pack_weights(w_ple_gate), _pack_weights(w_ple)

    p3d = p.reshape(depth, bsz * seq_len, -1)

    for i in range(depth):
        xs = _mixer_call(
            xs, row(g_mix[i]), win_p, rows8(conv4_w[i]), rows8(conv4_b[i]),
            wri_p, rows8(b_rg_r[i].reshape(-1)), rows8(b_rg_i[i].reshape(-1)), rows8(lru_lambda[i]), rows8(conv3_w[i]),
            wout_p, layer=i, seq_len=seq_len, permute_in=(i == 0))
        xs = _ffn_call(
            xs, p3d, row(g_ffn[i]), wgu_p, wd_p,
            row(g_ple[i]), wpg_p, wple_p,
            row(g_final), layer=i, final=(i == depth - 1))
    return xs.reshape(bsz, seq_len, d)
```

```python
import functools

import jax
import jax.numpy as jnp
from jax import lax
from jax.experimental import pallas as pl
from jax.experimental.pallas import tpu as pltpu

EPS = 1e-6
RG_C = 8.0
RG_HEADS = 4
CONV4_WIDTH = 4
CONV3_WIDTH = 3

SUBLANES = 8
LANES = 128
PACK_BLOCK_ROWS = 512
PACK_BLOCK_COLS = 2048
SUB = 128
J = SUB // SUBLANES
MIXER_GROUPS = 2
FFN_GROUPS = 4
FFN_SUB = 128
GATE_UP_BLOCK = 256
VMEM_LIMIT_BYTES = 60000 * 1024

_H3 = CONV3_WIDTH - 1
_H4 = CONV4_WIDTH - 1


def _to_stream_order(a):
    return jnp.swapaxes(a.reshape(SUBLANES, J, a.shape[-1]), 0, 1).reshape(a.shape)


def _to_time_order(a):
    return jnp.swapaxes(a.reshape(J, SUBLANES, a.shape[-1]), 0, 1).reshape(a.shape)


def _rmsnorm(xs, g):
    ms = jnp.mean(xs * xs, axis=-1, keepdims=True)
    return (xs * lax.rsqrt(ms + EPS)) * g


def _sigmoid(z):
    return 0.5 * jnp.tanh(0.5 * z) + 0.5


def _softplus(z):
    return jnp.maximum(z, 0.0) + jnp.log1p(jnp.exp(-jnp.abs(z)))


def _rows(j, n=1):
    return slice(j * SUBLANES, (j + n) * SUBLANES)


def _largest_block(size, unit, cap):
    return max(b for b in range(unit, min(size, cap) + 1, unit) if size % b == 0)


def _pack(w):
    return pltpu.bitcast(w.astype(jnp.bfloat16), jnp.uint32)


def _pack_kernel(w_ref, o_ref):
    o_ref[...] = _pack(w_ref[...])


def _pack_interleaved_kernel(a_ref, b_ref, o_ref, *, block):
    for t in range(a_ref.shape[-1] // block):
        src = slice(t * block, (t + 1) * block)
        o_ref[:, 2 * t * block:(2 * t + 1) * block] = _pack(a_ref[:, src])
        o_ref[:, (2 * t + 1) * block:(2 * t + 2) * block] = _pack(b_ref[:, src])


def _pack_weights(w, interleave_halves=None):
    layers, k, n = w.shape
    bk = _largest_block(k, 2 * SUBLANES, PACK_BLOCK_ROWS)
    out_shape = jax.ShapeDtypeStruct((layers, k // 2, n), jnp.uint32)
    if interleave_halves is None:
        bn = _largest_block(n, LANES, PACK_BLOCK_COLS)
        return pl.pallas_call(
            _pack_kernel,
            grid=(layers, k // bk, n // bn),
            in_specs=[pl.BlockSpec((None, bk, bn), lambda l, i, j: (l, i, j))],
            out_specs=pl.BlockSpec((None, bk // 2, bn), lambda l, i, j: (l, i, j)),
            out_shape=out_shape,
            name="pack_weights",
        )(w)
    bk = _largest_block(k, 2 * SUBLANES, PACK_BLOCK_ROWS // 2)
    half = n // 2
    return pl.pallas_call(
        functools.partial(_pack_interleaved_kernel, block=interleave_halves),
        grid=(layers, k // bk),
        in_specs=[pl.BlockSpec((None, bk, half), lambda l, i: (l, i, 0)),
                  pl.BlockSpec((None, bk, half), lambda l, i: (l, i, 1))],
        out_specs=pl.BlockSpec((None, bk // 2, n), lambda l, i: (l, i, 0)),
        out_shape=out_shape,
        name="pack_weights_interleaved",
    )(w, w)


def _weights(packed):
    return pltpu.bitcast(packed, jnp.bfloat16)


def _mixer_kernel(x_ref, xn_ref, gmix_ref, win_ref, c4w_ref, c4b_ref, wri_ref, br_ref,
                  bi_ref, lam_ref, c3w_ref, wout_ref, o_ref,
                  hn0_ref, hn1_ref, z0_ref, z1_ref, xc0_ref, xc1_ref, yc0_ref, yc1_ref,
                  pre_ref, a_ref, u_ref, hl_ref, pl_ref, mrg_ref,
                  p4_ref, pq_ref, hc_ref, c_ref, xp0_ref, xp1_ref, *,
                  groups, groups_per_seq, d, permute_in):
    g = pl.program_id(0)
    hb = d // RG_HEADS
    o_rx, o_ry, o_cb, o_cc, o_cx, o_gr, o_gc = (k * d for k in range(7))
    first_stream = lax.broadcasted_iota(jnp.int32, (SUBLANES, hb), 0) == 0
    heads = [slice(h * hb, (h + 1) * hb) for h in range(RG_HEADS)]
    chunks = [(o_rx, o_ry), (o_cc, o_gr), (o_ry, o_cc), (o_gr, o_gc + d)]

    def zcols(off, cols):
        return slice(off + cols.start, off + cols.stop)

    def from_prev_stream(cur, prev):
        return jnp.where(first_stream, pltpu.roll(prev, 1, 0), pltpu.roll(cur, 1, 0))

    def normalize(xs, hn_ref):
        hn_ref[...] = _rmsnorm(xs, gmix_ref[...]).astype(jnp.bfloat16)

    def project(hn_ref, z_ref, chunk):
        lo, hi = chunks[chunk]
        z_ref[:, lo:hi] = jnp.dot(hn_ref[...], _weights(win_ref[:, lo:hi]),
                                  preferred_element_type=jnp.float32)

    def convs(z_ref, xc_ref, yc_ref, cols, seq_start):
        def history(h_ref, jj):
            return jnp.where(seq_start, 0.0, h_ref[_rows(jj), cols])

        c_rx = zcols(o_rx, cols)
        halo4 = [from_prev_stream(z_ref[_rows(J - _H4 + jj), c_rx], history(p4_ref, jj))
                 for jj in range(_H4)]
        p4_ref[:, cols] = z_ref[_rows(J - _H4, _H4), c_rx]

        def rx(j):
            return halo4[j + _H4] if j < 0 else z_ref[_rows(j), c_rx]

        for j in range(J):
            acc = c4b_ref[:, cols] + c4w_ref[_rows(0), cols] * rx(j - _H4)
            for k in range(1, CONV4_WIDTH):
                acc = acc + c4w_ref[_rows(k), cols] * rx(j - _H4 + k)
            xc_ref[_rows(j), cols] = acc

        def q(j):
            return z_ref[_rows(j), zcols(o_cc, cols)] * z_ref[_rows(j), zcols(o_cx, cols)]

        tail = [q(J - _H3 + jj) for jj in range(_H3)]
        window = [from_prev_stream(tail[jj], history(pq_ref, jj)) for jj in range(_H3)]
        for jj in range(_H3):
            pq_ref[_rows(jj), cols] = tail[jj]
        for j in range(J):
            qj = q(j) if j < J - _H3 else tail[j - (J - _H3)]
            yc = c3w_ref[_rows(_H3), cols] * qj
            for k in range(_H3):
                yc = yc + c3w_ref[_rows(k), cols] * window[k]
            yc_ref[_rows(j), cols] = yc
            window = window[1:] + [qj]

    def gates(xc_ref, h, cols):
        xh = xc_ref[:, cols]
        xb = xh.astype(jnp.bfloat16)
        slot = lax.rem(g, 2)
        r_cols = slice(2 * cols.start, 2 * cols.start + hb)
        i_cols = slice(2 * cols.start + hb, 2 * cols.stop)
        pre_ref[slot, :, 2 * cols.start:2 * cols.stop] = jnp.dot(
            xb, _weights(wri_ref[h]), preferred_element_type=jnp.float32)
        half_scale = (-0.5 * RG_C) * _softplus(-lam_ref[:, cols])
        for j in range(J):
            t_r = jnp.tanh(0.5 * (pre_ref[slot, _rows(j), r_cols] + br_ref[:, cols]))
            gi = _sigmoid(pre_ref[slot, _rows(j), i_cols] + bi_ref[:, cols])
            log_a = half_scale * t_r + half_scale
            a = jnp.exp(log_a)
            w = -jnp.tanh(log_a) * (1.0 + a * a)
            mult = jnp.where(w > 0.0, w * lax.rsqrt(w), 0.0)
            a_ref[_rows(j), cols] = a
            u_ref[_rows(j), cols] = mult * (gi * xh[_rows(j), :])

    def scan_merge(z_ref, yc_ref, cols, seq_start):
        hloc = jnp.zeros((SUBLANES, hb), jnp.float32)
        prod = jnp.ones((SUBLANES, hb), jnp.float32)
        for j in range(J):
            a = a_ref[_rows(j), cols]
            hloc = a * hloc + u_ref[_rows(j), cols]
            prod = a * prod
        hl_ref[:, cols] = hloc
        pl_ref[:, cols] = prod

        cur = jnp.where(seq_start, 0.0, hc_ref[0:1, cols])
        for s in range(SUBLANES):
            c_ref[s:s + 1, cols] = cur
            cur = hl_ref[s:s + 1, cols] + pl_ref[s:s + 1, cols] * cur
        hc_ref[0:1, cols] = cur
        hstate = c_ref[:, cols]

        for j2 in range(J // 2):
            merged = []
            for j in (2 * j2, 2 * j2 + 1):
                hstate = a_ref[_rows(j), cols] * hstate + u_ref[_rows(j), cols]
                y_rnn = jax.nn.gelu(z_ref[_rows(j), zcols(o_ry, cols)]) * hstate
                y_conv = z_ref[_rows(j), zcols(o_cb, cols)] * yc_ref[_rows(j), cols]
                merged.append(_sigmoid(z_ref[_rows(j), zcols(o_gr, cols)]) * y_rnn
                              + _sigmoid(z_ref[_rows(j), zcols(o_gc, cols)]) * y_conv)
            mrg_ref[_rows(2 * j2, 2), cols] = jnp.concatenate(merged, axis=0).astype(jnp.bfloat16)

    def output(h, cols):
        w_rows = slice(cols.start // 2, cols.stop // 2)
        return jnp.dot(mrg_ref[:, cols], _weights(wout_ref[w_rows, :]),
                       preferred_element_type=jnp.float32)

    starts = [(groups * g + k) % groups_per_seq == 0 for k in range(groups + 1)]
    slots = ((hn0_ref, z0_ref, xc0_ref, yc0_ref), (hn1_ref, z1_ref, xc1_ref, yc1_ref))
    xp_slots = (xp0_ref, xp1_ref)

    def group_rows(k):
        xs = x_ref[k * SUB:(k + 1) * SUB, :] if k < groups else xn_ref[...]
        if not permute_in:
            return xs
        xp_ref = xp_slots[k % 2]
        xp_ref[...] = _to_stream_order(xs)
        return xp_ref[...]

    @pl.when(g == 0)
    def _():
        p4_ref[...] = jnp.zeros_like(p4_ref)
        pq_ref[...] = jnp.zeros_like(pq_ref)
        hc_ref[...] = jnp.zeros_like(hc_ref)
        normalize(group_rows(0), hn0_ref)
        for chunk in range(len(chunks)):
            project(hn0_ref, z0_ref, chunk)
        for cols in heads:
            convs(z0_ref, xc0_ref, yc0_ref, cols, True)

    for k in range(groups):
        rows = slice(k * SUB, (k + 1) * SUB)
        _, z_cur, xc_cur, yc_cur = slots[k % 2]
        hn_nxt, z_nxt, xc_nxt, yc_nxt = slots[(k + 1) % 2]
        normalize(group_rows(k + 1), hn_nxt)
        acc = xp_slots[k % 2][...] if permute_in else x_ref[rows, :]
        for h, cols in enumerate(heads):
            gates(xc_cur, h, cols)
            project(hn_nxt, z_nxt, h)
            scan_merge(z_cur, yc_cur, cols, starts[k])
            if h > 0:
                acc = acc + output(h - 1, heads[h - 1])
        o_ref[rows, :] = acc + output(RG_HEADS - 1, heads[-1])
        for cols in heads:
            convs(z_nxt, xc_nxt, yc_nxt, cols, starts[k + 1])


def _ffn_kernel(x_ref, xn_ref, p_ref, gffn_ref, wgu_ref, wd_ref, gple_ref, wpg_ref,
                wple_ref, gfin_ref, o_ref, act0_ref, act1_ref, *,
                groups, d_ff, final):
    g = pl.program_id(0)

    def project(xs, act_ref):
        hn = _rmsnorm(xs, gffn_ref[...]).astype(jnp.bfloat16)
        gu = jnp.dot(hn, _weights(wgu_ref[...]), preferred_element_type=jnp.float32)
        b = GATE_UP_BLOCK
        for t in range(d_ff // b):
            gate = gu[:, 2 * t * b:(2 * t + 1) * b]
            up = gu[:, (2 * t + 1) * b:(2 * t + 2) * b]
            act_ref[:, t * b:(t + 1) * b] = ((gate * _sigmoid(gate)) * up).astype(jnp.bfloat16)

    def down(act_ref, xs):
        return xs + jnp.dot(act_ref[...], _weights(wd_ref[...]), preferred_element_type=jnp.float32)

    def embed(x2, ps):
        hn2 = _rmsnorm(x2, gple_ref[...]).astype(jnp.bfloat16)
        gate = _sigmoid(jnp.dot(hn2, _weights(wpg_ref[...]), preferred_element_type=jnp.float32))
        pb = _to_stream_order(ps).astype(jnp.bfloat16)
        pe = jnp.dot(pb, _weights(wple_ref[...]), preferred_element_type=jnp.float32)
        x3 = x2 + gate * pe
        if final:
            x3 = _to_time_order(_rmsnorm(x3, gfin_ref[...]))
        return x3

    @pl.when(g == 0)
    def _():
        project(x_ref[0:FFN_SUB, :], act0_ref)

    act_slots = (act0_ref, act1_ref)
    for k in range(groups):
        rows = slice(k * FFN_SUB, (k + 1) * FFN_SUB)
        nxt = slice((k + 1) * FFN_SUB, (k + 2) * FFN_SUB)
        x2 = down(act_slots[k % 2], x_ref[rows, :])
        project(x_ref[nxt, :] if k + 1 < groups else xn_ref[...], act_slots[(k + 1) % 2])
        o_ref[rows, :] = embed(x2, p_ref[rows, :])


def _resident(shape):
    zeros = (0,) * len(shape)
    return pl.BlockSpec(shape, lambda g: zeros, pipeline_mode=pl.Buffered(1))


def _resident_layer(stacked_shape, layer):
    index = (layer,) + (0,) * (len(stacked_shape) - 1)
    return pl.BlockSpec((None,) + tuple(stacked_shape[1:]), lambda g: index,
                        pipeline_mode=pl.Buffered(1))


def _row_specs(n, d, sub, groups):
    last_group = n // sub - 1
    cur = pl.BlockSpec((groups * sub, d), lambda g: (g, 0))
    nxt = pl.BlockSpec((sub, d), lambda g: (jnp.minimum(groups * (g + 1), last_group), 0))
    return cur, nxt


def _mixer_call(x2d, gmix, win, c4w, c4b, wri, br, bi, lam, c3w, wout, *, layer, seq_len,
                permute_in):
    n, d = x2d.shape
    w_in_cols = win.shape[-1]
    f32 = jnp.float32
    groups = MIXER_GROUPS
    cur, nxt = _row_specs(n, d, SUB, groups)
    consts = (gmix, win, c4w, c4b, wri, br, bi, lam, c3w, wout)
    stacked = (win, wri, wout)
    return pl.pallas_call(
        functools.partial(_mixer_kernel, groups=groups, groups_per_seq=seq_len // SUB, d=d,
                          permute_in=permute_in),
        grid=(n // (groups * SUB),),
        in_specs=[cur, nxt] + [
            _resident_layer(c.shape, layer) if any(c is s for s in stacked)
            else _resident(c.shape) for c in consts],
        out_specs=cur,
        out_shape=jax.ShapeDtypeStruct((n, d), f32),
        scratch_shapes=[
            pltpu.VMEM((SUB, d), jnp.bfloat16),
            pltpu.VMEM((SUB, d), jnp.bfloat16),
            pltpu.VMEM((SUB, w_in_cols), f32),
            pltpu.VMEM((SUB, w_in_cols), f32),
            pltpu.VMEM((SUB, d), f32),
            pltpu.VMEM((SUB, d), f32),
            pltpu.VMEM((SUB, d), f32),
            pltpu.VMEM((SUB, d), f32),
            pltpu.VMEM((2, SUB, 2 * d), f32),
            pltpu.VMEM((SUB, d), f32),
            pltpu.VMEM((SUB, d), f32),
            pltpu.VMEM((SUBLANES, d), f32),
            pltpu.VMEM((SUBLANES, d), f32),
            pltpu.VMEM((SUB, d), jnp.bfloat16),
            pltpu.VMEM((_H4 * SUBLANES, d), f32),
            pltpu.VMEM((_H3 * SUBLANES, d), f32),
            pltpu.VMEM((SUBLANES, d), f32),
            pltpu.VMEM((SUBLANES, d), f32),
            pltpu.VMEM((SUB, d), f32),
            pltpu.VMEM((SUB, d), f32),
        ],
        compiler_params=pltpu.CompilerParams(
            dimension_semantics=("arbitrary",), vmem_limit_bytes=VMEM_LIMIT_BYTES),
        name="mixer",
    )(x2d, x2d, *consts)


def _ffn_call(x2d, p3d, gffn, wgu, wd, gple, wpg, wple, gfin, *, layer, final):
    n, d = x2d.shape
    d_ff = wgu.shape[-1] // 2
    groups, sub = FFN_GROUPS, FFN_SUB
    cur, nxt = _row_specs(n, d, sub, groups)
    p_spec = pl.BlockSpec((None, groups * sub, p3d.shape[-1]), lambda g: (layer, g, 0))
    consts = (gffn, wgu, wd, gple, wpg, wple, gfin)
    stacked = (wgu, wd, wpg, wple)
    return pl.pallas_call(
        functools.partial(_ffn_kernel, groups=groups, d_ff=d_ff, final=final),
        grid=(n // (groups * sub),),
        in_specs=[cur, nxt, p_spec] + [
            _resident_layer(c.shape, layer) if any(c is s for s in stacked)
            else _resident(c.shape) for c in consts],
        out_specs=cur,
        out_shape=jax.ShapeDtypeStruct((n, d), jnp.float32),
        scratch_shapes=[pltpu.VMEM((sub, d_ff), jnp.bfloat16),
                        pltpu.VMEM((sub, d_ff), jnp.bfloat16)],
        compiler_params=pltpu.CompilerParams(
            dimension_semantics=("arbitrary",), vmem_limit_bytes=VMEM_LIMIT_BYTES),
        name="ffn_ple",
    )(x2d, x2d, p3d, *consts)


def kernel(x, p, g_mix, w_in, conv4_w, conv4_b, w_rg_r, b_rg_r, w_rg_i, b_rg_i, lru_lambda, conv3_w, w_out, g_ffn, w_gate_up, w_down, g_ple, w_ple_gate, w_ple, g_final):
    bsz, seq_len, d = x.shape
    depth = p.shape[0]
    for groups, sub in ((MIXER_GROUPS, SUB), (FFN_GROUPS, FFN_SUB)):
        assert groups % 2 == 0 and seq_len % (groups * sub) == 0
    assert d % (RG_HEADS * LANES) == 0 and FFN_SUB == SUB
    row = lambda v: v.reshape(1, -1)
    rows8 = lambda v: jnp.repeat(v.reshape(-1, v.shape[-1]), SUBLANES, axis=0)

    xs = x.reshape(bsz * seq_len, d)
    win_p, wout_p = _pack_weights(w_in), _pack_weights(w_out)
    w_ri = jnp.concatenate([w_rg_r, w_rg_i], axis=-1)
    wri_p = _pack_weights(w_ri.reshape((-1,) + w_ri.shape[-2:])).reshape(
        w_ri.shape[:2] + (w_ri.shape[2] // 2, w_ri.shape[3]))
    wgu_p = _pack_weights(w_gate_up, interleave_halves=GATE_UP_BLOCK)
    wd_p = _pack_weights(w_down)
    wpg_p, wple_p = _pack_weights(w_ple_gate), _pack_weights(w_ple)

    p3d = p.reshape(depth, bsz * seq_len, -1)

    for i in range(depth):
        xs = _mixer_call(
            xs, row(g_mix[i]), win_p, rows8(conv4_w[i]), rows8(conv4_b[i]),
            wri_p, rows8(b_rg_r[i].reshape(-1)), rows8(b_rg_i[i].reshape(-1)), rows8(lru_lambda[i]), rows8(conv3_w[i]),
            wout_p, layer=i, seq_len=seq_len, permute_in=(i == 0))
        xs = _ffn_call(
            xs, p3d, row(g_ffn[i]), wgu_p, wd_p,
            row(g_ple[i]), wpg_p, wple_p,
            row(g_final), layer=i, final=(i == depth - 1))
    return xs.reshape(bsz, seq_len, d)
```

```python
import functools

import jax
import jax.numpy as jnp
from jax import lax
from jax.experimental import pallas as pl
from jax.experimental.pallas import tpu as pltpu

EPS = 1e-6
RG_C = 8.0
RG_HEADS = 4
CONV4_WIDTH = 4
CONV3_WIDTH = 3

SUBLANES = 8
LANES = 128
PACK_BLOCK_ROWS = 512
PACK_BLOCK_COLS = 2048
SUB = 128
J = SUB // SUBLANES
MIXER_GROUPS = 2
FFN_GROUPS = 8
FFN_SUB = 128
GATE_UP_BLOCK = 256
VMEM_LIMIT_BYTES = 60000 * 1024

_H3 = CONV3_WIDTH - 1
_H4 = CONV4_WIDTH - 1


def _to_stream_order(a):
    return jnp.swapaxes(a.reshape(SUBLANES, J, a.shape[-1]), 0, 1).reshape(a.shape)


def _to_time_order(a):
    return jnp.swapaxes(a.reshape(J, SUBLANES, a.shape[-1]), 0, 1).reshape(a.shape)


def _rmsnorm(xs, g):
    ms = jnp.mean(xs * xs, axis=-1, keepdims=True)
    return (xs * lax.rsqrt(ms + EPS)) * g


def _sigmoid(z):
    return 0.5 * jnp.tanh(0.5 * z) + 0.5


def _softplus(z):
    return jnp.maximum(z, 0.0) + jnp.log1p(jnp.exp(-jnp.abs(z)))


def _rows(j, n=1):
    return slice(j * SUBLANES, (j + n) * SUBLANES)


def _largest_block(size, unit, cap):
    return max(b for b in range(unit, min(size, cap) + 1, unit) if size % b == 0)


def _pack(w):
    return pltpu.bitcast(w.astype(jnp.bfloat16), jnp.uint32)


def _pack_kernel(w_ref, o_ref):
    o_ref[...] = _pack(w_ref[...])


def _pack_interleaved_kernel(a_ref, b_ref, o_ref, *, block):
    for t in range(a_ref.shape[-1] // block):
        src = slice(t * block, (t + 1) * block)
        o_ref[:, 2 * t * block:(2 * t + 1) * block] = _pack(a_ref[:, src])
        o_ref[:, (2 * t + 1) * block:(2 * t + 2) * block] = _pack(b_ref[:, src])


def _pack_weights(w, interleave_halves=None):
    layers, k, n = w.shape
    bk = _largest_block(k, 2 * SUBLANES, PACK_BLOCK_ROWS)
    out_shape = jax.ShapeDtypeStruct((layers, k // 2, n), jnp.uint32)
    if interleave_halves is None:
        bn = _largest_block(n, LANES, PACK_BLOCK_COLS)
        return pl.pallas_call(
            _pack_kernel,
            grid=(layers, k // bk, n // bn),
            in_specs=[pl.BlockSpec((None, bk, bn), lambda l, i, j: (l, i, j))],
            out_specs=pl.BlockSpec((None, bk // 2, bn), lambda l, i, j: (l, i, j)),
            out_shape=out_shape,
            name="pack_weights",
        )(w)
    bk = _largest_block(k, 2 * SUBLANES, PACK_BLOCK_ROWS // 2)
    half = n // 2
    return pl.pallas_call(
        functools.partial(_pack_interleaved_kernel, block=interleave_halves),
        grid=(layers, k // bk),
        in_specs=[pl.BlockSpec((None, bk, half), lambda l, i: (l, i, 0)),
                  pl.BlockSpec((None, bk, half), lambda l, i: (l, i, 1))],
        out_specs=pl.BlockSpec((None, bk // 2, n), lambda l, i: (l, i, 0)),
        out_shape=out_shape,
        name="pack_weights_interleaved",
    )(w, w)


def _weights(packed):
    return pltpu.bitcast(packed, jnp.bfloat16)


def _mixer_kernel(x_ref, xn_ref, gmix_ref, win_ref, c4w_ref, c4b_ref, wri_ref, br_ref,
                  bi_ref, lam_ref, c3w_ref, wout_ref, o_ref,
                  hn0_ref, hn1_ref, z0_ref, z1_ref, xc0_ref, xc1_ref, yc0_ref, yc1_ref,
                  pre_ref, a_ref, u_ref, hl_ref, pl_ref, mrg_ref,
                  p4_ref, pq_ref, hc_ref, c_ref, xp0_ref, xp1_ref, *,
                  groups, groups_per_seq, d, permute_in):
    g = pl.program_id(0)
    hb = d // RG_HEADS
    o_rx, o_ry, o_cb, o_cc, o_cx, o_gr, o_gc = (k * d for k in range(7))
    first_stream = lax.broadcasted_iota(jnp.int32, (SUBLANES, hb), 0) == 0
    heads = [slice(h * hb, (h + 1) * hb) for h in range(RG_HEADS)]
    chunks = [(o_rx, o_ry), (o_cc, o_gr), (o_ry, o_cc), (o_gr, o_gc + d)]

    def zcols(off, cols):
        return slice(off + cols.start, off + cols.stop)

    def from_prev_stream(cur, prev):
        return jnp.where(first_stream, pltpu.roll(prev, 1, 0), pltpu.roll(cur, 1, 0))

    def normalize(xs, hn_ref):
        hn_ref[...] = _rmsnorm(xs, gmix_ref[...]).astype(jnp.bfloat16)

    def project(hn_ref, z_ref, chunk):
        lo, hi = chunks[chunk]
        z_ref[:, lo:hi] = jnp.dot(hn_ref[...], _weights(win_ref[:, lo:hi]),
                                  preferred_element_type=jnp.float32)

    def convs(z_ref, xc_ref, yc_ref, cols, seq_start):
        def history(h_ref, jj):
            return jnp.where(seq_start, 0.0, h_ref[_rows(jj), cols])

        c_rx = zcols(o_rx, cols)
        halo4 = [from_prev_stream(z_ref[_rows(J - _H4 + jj), c_rx], history(p4_ref, jj))
                 for jj in range(_H4)]
        p4_ref[:, cols] = z_ref[_rows(J - _H4, _H4), c_rx]

        def rx(j):
            return halo4[j + _H4] if j < 0 else z_ref[_rows(j), c_rx]

        for j in range(J):
            acc = c4b_ref[:, cols] + c4w_ref[_rows(0), cols] * rx(j - _H4)
            for k in range(1, CONV4_WIDTH):
                acc = acc + c4w_ref[_rows(k), cols] * rx(j - _H4 + k)
            xc_ref[_rows(j), cols] = acc

        def q(j):
            return z_ref[_rows(j), zcols(o_cc, cols)] * z_ref[_rows(j), zcols(o_cx, cols)]

        tail = [q(J - _H3 + jj) for jj in range(_H3)]
        window = [from_prev_stream(tail[jj], history(pq_ref, jj)) for jj in range(_H3)]
        for jj in range(_H3):
            pq_ref[_rows(jj), cols] = tail[jj]
        for j in range(J):
            qj = q(j) if j < J - _H3 else tail[j - (J - _H3)]
            yc = c3w_ref[_rows(_H3), cols] * qj
            for k in range(_H3):
                yc = yc + c3w_ref[_rows(k), cols] * window[k]
            yc_ref[_rows(j), cols] = yc
            window = window[1:] + [qj]

    def gates(xc_ref, h, cols):
        xh = xc_ref[:, cols]
        xb = xh.astype(jnp.bfloat16)
        slot = lax.rem(g, 2)
        r_cols = slice(2 * cols.start, 2 * cols.start + hb)
        i_cols = slice(2 * cols.start + hb, 2 * cols.stop)
        pre_ref[slot, :, 2 * cols.start:2 * cols.stop] = jnp.dot(
            xb, _weights(wri_ref[h]), preferred_element_type=jnp.float32)
        half_scale = (-0.5 * RG_C) * _softplus(-lam_ref[:, cols])
        for j in range(J):
            t_r = jnp.tanh(0.5 * (pre_ref[slot, _rows(j), r_cols] + br_ref[:, cols]))
            gi = _sigmoid(pre_ref[slot, _rows(j), i_cols] + bi_ref[:, cols])
            log_a = half_scale * t_r + half_scale
            a = jnp.exp(log_a)
            w = -jnp.tanh(log_a) * (1.0 + a * a)
            mult = jnp.where(w > 0.0, w * lax.rsqrt(w), 0.0)
            a_ref[_rows(j), cols] = a
            u_ref[_rows(j), cols] = mult * (gi * xh[_rows(j), :])

    def scan_merge(z_ref, yc_ref, cols, seq_start):
        hloc = jnp.zeros((SUBLANES, hb), jnp.float32)
        prod = jnp.ones((SUBLANES, hb), jnp.float32)
        for j in range(J):
            a = a_ref[_rows(j), cols]
            hloc = a * hloc + u_ref[_rows(j), cols]
            prod = a * prod
        hl_ref[:, cols] = hloc
        pl_ref[:, cols] = prod

        cur = jnp.where(seq_start, 0.0, hc_ref[0:1, cols])
        for s in range(SUBLANES):
            c_ref[s:s + 1, cols] = cur
            cur = hl_ref[s:s + 1, cols] + pl_ref[s:s + 1, cols] * cur
        hc_ref[0:1, cols] = cur
        hstate = c_ref[:, cols]

        for j2 in range(J // 2):
            merged = []
            for j in (2 * j2, 2 * j2 + 1):
                hstate = a_ref[_rows(j), cols] * hstate + u_ref[_rows(j), cols]
                y_rnn = jax.nn.gelu(z_ref[_rows(j), zcols(o_ry, cols)]) * hstate
                y_conv = z_ref[_rows(j), zcols(o_cb, cols)] * yc_ref[_rows(j), cols]
                merged.append(_sigmoid(z_ref[_rows(j), zcols(o_gr, cols)]) * y_rnn
                              + _sigmoid(z_ref[_rows(j), zcols(o_gc, cols)]) * y_conv)
            mrg_ref[_rows(2 * j2, 2), cols] = jnp.concatenate(merged, axis=0).astype(jnp.bfloat16)

    def output(h, cols):
        w_rows = slice(cols.start // 2, cols.stop // 2)
        return jnp.dot(mrg_ref[:, cols], _weights(wout_ref[w_rows, :]),
                       preferred_element_type=jnp.float32)

    starts = [(groups * g + k) % groups_per_seq == 0 for k in range(groups + 1)]
    slots = ((hn0_ref, z0_ref, xc0_ref, yc0_ref), (hn1_ref, z1_ref, xc1_ref, yc1_ref))
    xp_slots = (xp0_ref, xp1_ref)

    def group_rows(k):
        xs = x_ref[k * SUB:(k + 1) * SUB, :] if k < groups else xn_ref[...]
        if not permute_in:
            return xs
        xp_ref = xp_slots[k % 2]
        xp_ref[...] = _to_stream_order(xs)
        return xp_ref[...]

    @pl.when(g == 0)
    def _():
        p4_ref[...] = jnp.zeros_like(p4_ref)
        pq_ref[...] = jnp.zeros_like(pq_ref)
        hc_ref[...] = jnp.zeros_like(hc_ref)
        normalize(group_rows(0), hn0_ref)
        for chunk in range(len(chunks)):
            project(hn0_ref, z0_ref, chunk)
        for cols in heads:
            convs(z0_ref, xc0_ref, yc0_ref, cols, True)

    for k in range(groups):
        rows = slice(k * SUB, (k + 1) * SUB)
        _, z_cur, xc_cur, yc_cur = slots[k % 2]
        hn_nxt, z_nxt, xc_nxt, yc_nxt = slots[(k + 1) % 2]
        normalize(group_rows(k + 1), hn_nxt)
        acc = xp_slots[k % 2][...] if permute_in else x_ref[rows, :]
        for h, cols in enumerate(heads):
            gates(xc_cur, h, cols)
            project(hn_nxt, z_nxt, h)
            scan_merge(z_cur, yc_cur, cols, starts[k])
            if h > 0:
                acc = acc + output(h - 1, heads[h - 1])
        o_ref[rows, :] = acc + output(RG_HEADS - 1, heads[-1])
        for cols in heads:
            convs(z_nxt, xc_nxt, yc_nxt, cols, starts[k + 1])


def _ffn_kernel(x_ref, xn_ref, p_ref, gffn_ref, wgu_ref, wd_ref, gple_ref, wpg_ref,
                wple_ref, gfin_ref, o_ref, act0_ref, act1_ref, *,
                groups, d_ff, final):
    g = pl.program_id(0)

    def project(xs, act_ref):
        hn = _rmsnorm(xs, gffn_ref[...]).astype(jnp.bfloat16)
        gu = jnp.dot(hn, _weights(wgu_ref[...]), preferred_element_type=jnp.float32)
        b = GATE_UP_BLOCK
        for t in range(d_ff // b):
            gate = gu[:, 2 * t * b:(2 * t + 1) * b]
            up = gu[:, (2 * t + 1) * b:(2 * t + 2) * b]
            act_ref[:, t * b:(t + 1) * b] = ((gate * _sigmoid(gate)) * up).astype(jnp.bfloat16)

    def down(act_ref, xs):
        return xs + jnp.dot(act_ref[...], _weights(wd_ref[...]), preferred_element_type=jnp.float32)

    def embed(x2, ps):
        hn2 = _rmsnorm(x2, gple_ref[...]).astype(jnp.bfloat16)
        gate = _sigmoid(jnp.dot(hn2, _weights(wpg_ref[...]), preferred_element_type=jnp.float32))
        pb = _to_stream_order(ps).astype(jnp.bfloat16)
        pe = jnp.dot(pb, _weights(wple_ref[...]), preferred_element_type=jnp.float32)
        x3 = x2 + gate * pe
        if final:
            x3 = _to_time_order(_rmsnorm(x3, gfin_ref[...]))
        return x3

    @pl.when(g == 0)
    def _():
        project(x_ref[0:FFN_SUB, :], act0_ref)

    act_slots = (act0_ref, act1_ref)
    for k in range(groups):
        rows = slice(k * FFN_SUB, (k + 1) * FFN_SUB)
        nxt = slice((k + 1) * FFN_SUB, (k + 2) * FFN_SUB)
        x2 = down(act_slots[k % 2], x_ref[rows, :])
        project(x_ref[nxt, :] if k + 1 < groups else xn_ref[...], act_slots[(k + 1) % 2])
        o_ref[rows, :] = embed(x2, p_ref[rows, :])


def _resident(shape):
    zeros = (0,) * len(shape)
    return pl.BlockSpec(shape, lambda g: zeros, pipeline_mode=pl.Buffered(1))


def _resident_layer(stacked_shape, layer):
    index = (layer,) + (0,) * (len(stacked_shape) - 1)
    return pl.BlockSpec((None,) + tuple(stacked_shape[1:]), lambda g: index,
                        pipeline_mode=pl.Buffered(1))


def _row_specs(n, d, sub, groups):
    last_group = n // sub - 1
    cur = pl.BlockSpec((groups * sub, d), lambda g: (g, 0))
    nxt = pl.BlockSpec((sub, d), lambda g: (jnp.minimum(groups * (g + 1), last_group), 0))
    return cur, nxt


def _mixer_call(x2d, gmix, win, c4w, c4b, wri, br, bi, lam, c3w, wout, *, layer, seq_len,
                permute_in):
    n, d = x2d.shape
    w_in_cols = win.shape[-1]
    f32 = jnp.float32
    groups = MIXER_GROUPS
    cur, nxt = _row_specs(n, d, SUB, groups)
    consts = (gmix, win, c4w, c4b, wri, br, bi, lam, c3w, wout)
    stacked = (win, wri, wout)
    return pl.pallas_call(
        functools.partial(_mixer_kernel, groups=groups, groups_per_seq=seq_len // SUB, d=d,
                          permute_in=permute_in),
        grid=(n // (groups * SUB),),
        in_specs=[cur, nxt] + [
            _resident_layer(c.shape, layer) if any(c is s for s in stacked)
            else _resident(c.shape) for c in consts],
        out_specs=cur,
        out_shape=jax.ShapeDtypeStruct((n, d), f32),
        scratch_shapes=[
            pltpu.VMEM((SUB, d), jnp.bfloat16),
            pltpu.VMEM((SUB, d), jnp.bfloat16),
            pltpu.VMEM((SUB, w_in_cols), f32),
            pltpu.VMEM((SUB, w_in_cols), f32),
            pltpu.VMEM((SUB, d), f32),
            pltpu.VMEM((SUB, d), f32),
            pltpu.VMEM((SUB, d), f32),
            pltpu.VMEM((SUB, d), f32),
            pltpu.VMEM((2, SUB, 2 * d), f32),
            pltpu.VMEM((SUB, d), f32),
            pltpu.VMEM((SUB, d), f32),
            pltpu.VMEM((SUBLANES, d), f32),
            pltpu.VMEM((SUBLANES, d), f32),
            pltpu.VMEM((SUB, d), jnp.bfloat16),
            pltpu.VMEM((_H4 * SUBLANES, d), f32),
            pltpu.VMEM((_H3 * SUBLANES, d), f32),
            pltpu.VMEM((SUBLANES, d), f32),
            pltpu.VMEM((SUBLANES, d), f32),
            pltpu.VMEM((SUB, d), f32),
            pltpu.VMEM((SUB, d), f32),
        ],
        compiler_params=pltpu.CompilerParams(
            dimension_semantics=("arbitrary",), vmem_limit_bytes=VMEM_LIMIT_BYTES),
        name="mixer",
    )(x2d, x2d, *consts)


def _ffn_call(x2d, p3d, gffn, wgu, wd, gple, wpg, wple, gfin, *, layer, final):
    n, d = x2d.shape
    d_ff = wgu.shape[-1] // 2
    groups, sub = FFN_GROUPS, FFN_SUB
    cur, nxt = _row_specs(n, d, sub, groups)
    p_spec = pl.BlockSpec((None, groups * sub, p3d.shape[-1]), lambda g: (layer, g, 0))
    consts = (gffn, wgu, wd, gple, wpg, wple, gfin)
    stacked = (wgu, wd, wpg, wple)
    return pl.pallas_call(
        functools.partial(_ffn_kernel, groups=groups, d_ff=d_ff, final=final),
        grid=(n // (groups * sub),),
        in_specs=[cur, nxt, p_spec] + [
            _resident_layer(c.shape, layer) if any(c is s for s in stacked)
            else _resident(c.shape) for c in consts],
        out_specs=cur,
        out_shape=jax.ShapeDtypeStruct((n, d), jnp.float32),
        scratch_shapes=[pltpu.VMEM((sub, d_ff), jnp.bfloat16),
                        pltpu.VMEM((sub, d_ff), jnp.bfloat16)],
        compiler_params=pltpu.CompilerParams(
            dimension_semantics=("arbitrary",), vmem_limit_bytes=VMEM_LIMIT_BYTES),
        name="ffn_ple",
    )(x2d, x2d, p3d, *consts)


def kernel(x, p, g_mix, w_in, conv4_w, conv4_b, w_rg_r, b_rg_r, w_rg_i, b_rg_i, lru_lambda, conv3_w, w_out, g_ffn, w_gate_up, w_down, g_ple, w_ple_gate, w_ple, g_final):
    bsz, seq_len, d = x.shape
    depth = p.shape[0]
    for groups, sub in ((MIXER_GROUPS, SUB), (FFN_GROUPS, FFN_SUB)):
        assert groups % 2 == 0 and seq_len % (groups * sub) == 0
    assert d % (RG_HEADS * LANES) == 0 and FFN_SUB == SUB
    row = lambda v: v.reshape(1, -1)
    rows8 = lambda v: jnp.repeat(v.reshape(-1, v.shape[-1]), SUBLANES, axis=0)

    xs = x.reshape(bsz * seq_len, d)
    win_p, wout_p = _pack_weights(w_in), _pack_weights(w_out)
    w_ri = jnp.concatenate([w_rg_r, w_rg_i], axis=-1)
    wri_p = _pack_weights(w_ri.reshape((-1,) + w_ri.shape[-2:])).reshape(
        w_ri.shape[:2] + (w_ri.shape[2] // 2, w_ri.shape[3]))
    wgu_p = _pack_weights(w_gate_up, interleave_halves=GATE_UP_BLOCK)
    wd_p = _pack_weights(w_down)
    wpg_p, wple_p = _pack_weights(w_ple_gate), _pack_weights(w_ple)

    p3d = p.reshape(depth, bsz * seq_len, -1)

    for i in range(depth):
        xs = _mixer_call(
            xs, row(g_mix[i]), win_p, rows8(conv4_w[i]), rows8(conv4_b[i]),
            wri_p, rows8(b_rg_r[i].reshape(-1)), rows8(b_rg_i[i].reshape(-1)), rows8(lru_lambda[i]), rows8(conv3_w[i]),
            wout_p, layer=i, seq_len=seq_len, permute_in=(i == 0))
        xs = _ffn_call(
            xs, p3d, row(g_ffn[i]), wgu_p, wd_p,
            row(g_ple[i]), wpg_p, wple_p,
            row(g_final), layer=i, final=(i == depth - 1))
    return xs.reshape(bsz, seq_len, d)
```

```python
import functools

import jax
import jax.numpy as jnp
from jax import lax
from jax.experimental import pallas as pl
from jax.experimental.pallas import tpu as pltpu

EPS = 1e-6
RG_C = 8.0
RG_HEADS = 4
CONV4_WIDTH = 4
CONV3_WIDTH = 3

SUBLANES = 8
LANES = 128
PACK_BLOCK_ROWS = 512
PACK_BLOCK_COLS = 2048
SUB = 128
J = SUB // SUBLANES
MIXER_GROUPS = 4
FFN_GROUPS = 8
FFN_SUB = 128
GATE_UP_BLOCK = 256
VMEM_LIMIT_BYTES = 60000 * 1024

_H3 = CONV3_WIDTH - 1
_H4 = CONV4_WIDTH - 1


def _to_stream_order(a):
    return jnp.swapaxes(a.reshape(SUBLANES, J, a.shape[-1]), 0, 1).reshape(a.shape)


def _to_time_order(a):
    return jnp.swapaxes(a.reshape(J, SUBLANES, a.shape[-1]), 0, 1).reshape(a.shape)


def _rmsnorm(xs, g):
    ms = jnp.mean(xs * xs, axis=-1, keepdims=True)
    return (xs * lax.rsqrt(ms + EPS)) * g


def _sigmoid(z):
    return 0.5 * jnp.tanh(0.5 * z) + 0.5


def _softplus(z):
    return jnp.maximum(z, 0.0) + jnp.log1p(jnp.exp(-jnp.abs(z)))


def _rows(j, n=1):
    return slice(j * SUBLANES, (j + n) * SUBLANES)


def _largest_block(size, unit, cap):
    return max(b for b in range(unit, min(size, cap) + 1, unit) if size % b == 0)


def _pack(w):
    return pltpu.bitcast(w.astype(jnp.bfloat16), jnp.uint32)


def _pack_kernel(w_ref, o_ref):
    o_ref[...] = _pack(w_ref[...])


def _pack_interleaved_kernel(a_ref, b_ref, o_ref, *, block):
    for t in range(a_ref.shape[-1] // block):
        src = slice(t * block, (t + 1) * block)
        o_ref[:, 2 * t * block:(2 * t + 1) * block] = _pack(a_ref[:, src])
        o_ref[:, (2 * t + 1) * block:(2 * t + 2) * block] = _pack(b_ref[:, src])


def _pack_weights(w, interleave_halves=None):
    layers, k, n = w.shape
    bk = _largest_block(k, 2 * SUBLANES, PACK_BLOCK_ROWS)
    out_shape = jax.ShapeDtypeStruct((layers, k // 2, n), jnp.uint32)
    if interleave_halves is None:
        bn = _largest_block(n, LANES, PACK_BLOCK_COLS)
        return pl.pallas_call(
            _pack_kernel,
            grid=(layers, k // bk, n // bn),
            in_specs=[pl.BlockSpec((None, bk, bn), lambda l, i, j: (l, i, j))],
            out_specs=pl.BlockSpec((None, bk // 2, bn), lambda l, i, j: (l, i, j)),
            out_shape=out_shape,
            name="pack_weights",
        )(w)
    bk = _largest_block(k, 2 * SUBLANES, PACK_BLOCK_ROWS // 2)
    half = n // 2
    return pl.pallas_call(
        functools.partial(_pack_interleaved_kernel, block=interleave_halves),
        grid=(layers, k // bk),
        in_specs=[pl.BlockSpec((None, bk, half), lambda l, i: (l, i, 0)),
                  pl.BlockSpec((None, bk, half), lambda l, i: (l, i, 1))],
        out_specs=pl.BlockSpec((None, bk // 2, n), lambda l, i: (l, i, 0)),
        out_shape=out_shape,
        name="pack_weights_interleaved",
    )(w, w)


def _weights(packed):
    return pltpu.bitcast(packed, jnp.bfloat16)


def _mixer_kernel(x_ref, xn_ref, gmix_ref, win_ref, c4w_ref, c4b_ref, wri_ref, br_ref,
                  bi_ref, lam_ref, c3w_ref, wout_ref, o_ref,
                  hn0_ref, hn1_ref, z0_ref, z1_ref, xc0_ref, xc1_ref, yc0_ref, yc1_ref,
                  pre_ref, a_ref, u_ref, hl_ref, pl_ref, mrg_ref,
                  p4_ref, pq_ref, hc_ref, c_ref, xp0_ref, xp1_ref, *,
                  groups, groups_per_seq, d, permute_in):
    g = pl.program_id(0)
    hb = d // RG_HEADS
    o_rx, o_ry, o_cb, o_cc, o_cx, o_gr, o_gc = (k * d for k in range(7))
    first_stream = lax.broadcasted_iota(jnp.int32, (SUBLANES, hb), 0) == 0
    heads = [slice(h * hb, (h + 1) * hb) for h in range(RG_HEADS)]
    chunks = [(o_rx, o_ry), (o_cc, o_gr), (o_ry, o_cc), (o_gr, o_gc + d)]

    def zcols(off, cols):
        return slice(off + cols.start, off + cols.stop)

    def from_prev_stream(cur, prev):
        return jnp.where(first_stream, pltpu.roll(prev, 1, 0), pltpu.roll(cur, 1, 0))

    def normalize(xs, hn_ref):
        hn_ref[...] = _rmsnorm(xs, gmix_ref[...]).astype(jnp.bfloat16)

    def project(hn_ref, z_ref, chunk):
        lo, hi = chunks[chunk]
        z_ref[:, lo:hi] = jnp.dot(hn_ref[...], _weights(win_ref[:, lo:hi]),
                                  preferred_element_type=jnp.float32)

    def convs(z_ref, xc_ref, yc_ref, cols, seq_start):
        def history(h_ref, jj):
            return jnp.where(seq_start, 0.0, h_ref[_rows(jj), cols])

        c_rx = zcols(o_rx, cols)
        halo4 = [from_prev_stream(z_ref[_rows(J - _H4 + jj), c_rx], history(p4_ref, jj))
                 for jj in range(_H4)]
        p4_ref[:, cols] = z_ref[_rows(J - _H4, _H4), c_rx]

        def rx(j):
            return halo4[j + _H4] if j < 0 else z_ref[_rows(j), c_rx]

        for j in range(J):
            acc = c4b_ref[:, cols] + c4w_ref[_rows(0), cols] * rx(j - _H4)
            for k in range(1, CONV4_WIDTH):
                acc = acc + c4w_ref[_rows(k), cols] * rx(j - _H4 + k)
            xc_ref[_rows(j), cols] = acc

        def q(j):
            return z_ref[_rows(j), zcols(o_cc, cols)] * z_ref[_rows(j), zcols(o_cx, cols)]

        tail = [q(J - _H3 + jj) for jj in range(_H3)]
        window = [from_prev_stream(tail[jj], history(pq_ref, jj)) for jj in range(_H3)]
        for jj in range(_H3):
            pq_ref[_rows(jj), cols] = tail[jj]
        for j in range(J):
            qj = q(j) if j < J - _H3 else tail[j - (J - _H3)]
            yc = c3w_ref[_rows(_H3), cols] * qj
            for k in range(_H3):
                yc = yc + c3w_ref[_rows(k), cols] * window[k]
            yc_ref[_rows(j), cols] = yc
            window = window[1:] + [qj]

    def gates(xc_ref, h, cols):
        xh = xc_ref[:, cols]
        xb = xh.astype(jnp.bfloat16)
        slot = lax.rem(g, 2)
        r_cols = slice(2 * cols.start, 2 * cols.start + hb)
        i_cols = slice(2 * cols.start + hb, 2 * cols.stop)
        pre_ref[slot, :, 2 * cols.start:2 * cols.stop] = jnp.dot(
            xb, _weights(wri_ref[h]), preferred_element_type=jnp.float32)
        half_scale = (-0.5 * RG_C) * _softplus(-lam_ref[:, cols])
        for j in range(J):
            t_r = jnp.tanh(0.5 * (pre_ref[slot, _rows(j), r_cols] + br_ref[:, cols]))
            gi = _sigmoid(pre_ref[slot, _rows(j), i_cols] + bi_ref[:, cols])
            log_a = half_scale * t_r + half_scale
            a = jnp.exp(log_a)
            w = -jnp.tanh(log_a) * (1.0 + a * a)
            mult = jnp.where(w > 0.0, w * lax.rsqrt(w), 0.0)
            a_ref[_rows(j), cols] = a
            u_ref[_rows(j), cols] = mult * (gi * xh[_rows(j), :])

    def scan_merge(z_ref, yc_ref, cols, seq_start):
        hloc = jnp.zeros((SUBLANES, hb), jnp.float32)
        prod = jnp.ones((SUBLANES, hb), jnp.float32)
        for j in range(J):
            a = a_ref[_rows(j), cols]
            hloc = a * hloc + u_ref[_rows(j), cols]
            prod = a * prod
        hl_ref[:, cols] = hloc
        pl_ref[:, cols] = prod

        cur = jnp.where(seq_start, 0.0, hc_ref[0:1, cols])
        for s in range(SUBLANES):
            c_ref[s:s + 1, cols] = cur
            cur = hl_ref[s:s + 1, cols] + pl_ref[s:s + 1, cols] * cur
        hc_ref[0:1, cols] = cur
        hstate = c_ref[:, cols]

        for j2 in range(J // 2):
            merged = []
            for j in (2 * j2, 2 * j2 + 1):
                hstate = a_ref[_rows(j), cols] * hstate + u_ref[_rows(j), cols]
                y_rnn = jax.nn.gelu(z_ref[_rows(j), zcols(o_ry, cols)]) * hstate
                y_conv = z_ref[_rows(j), zcols(o_cb, cols)] * yc_ref[_rows(j), cols]
                merged.append(_sigmoid(z_ref[_rows(j), zcols(o_gr, cols)]) * y_rnn
                              + _sigmoid(z_ref[_rows(j), zcols(o_gc, cols)]) * y_conv)
            mrg_ref[_rows(2 * j2, 2), cols] = jnp.concatenate(merged, axis=0).astype(jnp.bfloat16)

    def output(h, cols):
        w_rows = slice(cols.start // 2, cols.stop // 2)
        return jnp.dot(mrg_ref[:, cols], _weights(wout_ref[w_rows, :]),
                       preferred_element_type=jnp.float32)

    starts = [(groups * g + k) % groups_per_seq == 0 for k in range(groups + 1)]
    slots = ((hn0_ref, z0_ref, xc0_ref, yc0_ref), (hn1_ref, z1_ref, xc1_ref, yc1_ref))
    xp_slots = (xp0_ref, xp1_ref)

    def group_rows(k):
        xs = x_ref[k * SUB:(k + 1) * SUB, :] if k < groups else xn_ref[...]
        if not permute_in:
            return xs
        xp_ref = xp_slots[k % 2]
        xp_ref[...] = _to_stream_order(xs)
        return xp_ref[...]

    @pl.when(g == 0)
    def _():
        p4_ref[...] = jnp.zeros_like(p4_ref)
        pq_ref[...] = jnp.zeros_like(pq_ref)
        hc_ref[...] = jnp.zeros_like(hc_ref)
        normalize(group_rows(0), hn0_ref)
        for chunk in range(len(chunks)):
            project(hn0_ref, z0_ref, chunk)
        for cols in heads:
            convs(z0_ref, xc0_ref, yc0_ref, cols, True)

    for k in range(groups):
        rows = slice(k * SUB, (k + 1) * SUB)
        _, z_cur, xc_cur, yc_cur = slots[k % 2]
        hn_nxt, z_nxt, xc_nxt, yc_nxt = slots[(k + 1) % 2]
        normalize(group_rows(k + 1), hn_nxt)
        acc = xp_slots[k % 2][...] if permute_in else x_ref[rows, :]
        for h, cols in enumerate(heads):
            gates(xc_cur, h, cols)
            project(hn_nxt, z_nxt, h)
            scan_merge(z_cur, yc_cur, cols, starts[k])
            if h > 0:
                acc = acc + output(h - 1, heads[h - 1])
        o_ref[rows, :] = acc + output(RG_HEADS - 1, heads[-1])
        for cols in heads:
            convs(z_nxt, xc_nxt, yc_nxt, cols, starts[k + 1])


def _ffn_kernel(x_ref, xn_ref, p_ref, gffn_ref, wgu_ref, wd_ref, gple_ref, wpg_ref,
                wple_ref, gfin_ref, o_ref, act0_ref, act1_ref, *,
                groups, d_ff, final):
    g = pl.program_id(0)

    def project(xs, act_ref):
        hn = _rmsnorm(xs, gffn_ref[...]).astype(jnp.bfloat16)
        gu = jnp.dot(hn, _weights(wgu_ref[...]), preferred_element_type=jnp.float32)
        b = GATE_UP_BLOCK
        for t in range(d_ff // b):
            gate = gu[:, 2 * t * b:(2 * t + 1) * b]
            up = gu[:, (2 * t + 1) * b:(2 * t + 2) * b]
            act_ref[:, t * b:(t + 1) * b] = ((gate * _sigmoid(gate)) * up).astype(jnp.bfloat16)

    def down(act_ref, xs):
        return xs + jnp.dot(act_ref[...], _weights(wd_ref[...]), preferred_element_type=jnp.float32)

    def embed(x2, ps):
        hn2 = _rmsnorm(x2, gple_ref[...]).astype(jnp.bfloat16)
        gate = _sigmoid(jnp.dot(hn2, _weights(wpg_ref[...]), preferred_element_type=jnp.float32))
        pb = _to_stream_order(ps).astype(jnp.bfloat16)
        pe = jnp.dot(pb, _weights(wple_ref[...]), preferred_element_type=jnp.float32)
        x3 = x2 + gate * pe
        if final:
            x3 = _to_time_order(_rmsnorm(x3, gfin_ref[...]))
        return x3

    @pl.when(g == 0)
    def _():
        project(x_ref[0:FFN_SUB, :], act0_ref)

    act_slots = (act0_ref, act1_ref)
    for k in range(groups):
        rows = slice(k * FFN_SUB, (k + 1) * FFN_SUB)
        nxt = slice((k + 1) * FFN_SUB, (k + 2) * FFN_SUB)
        x2 = down(act_slots[k % 2], x_ref[rows, :])
        project(x_ref[nxt, :] if k + 1 < groups else xn_ref[...], act_slots[(k + 1) % 2])
        o_ref[rows, :] = embed(x2, p_ref[rows, :])


def _resident(shape):
    zeros = (0,) * len(shape)
    return pl.BlockSpec(shape, lambda g: zeros, pipeline_mode=pl.Buffered(1))


def _resident_layer(stacked_shape, layer):
    index = (layer,) + (0,) * (len(stacked_shape) - 1)
    return pl.BlockSpec((None,) + tuple(stacked_shape[1:]), lambda g: index,
                        pipeline_mode=pl.Buffered(1))


def _row_specs(n, d, sub, groups):
    last_group = n // sub - 1
    cur = pl.BlockSpec((groups * sub, d), lambda g: (g, 0))
    nxt = pl.BlockSpec((sub, d), lambda g: (jnp.minimum(groups * (g + 1), last_group), 0))
    return cur, nxt


def _mixer_call(x2d, gmix, win, c4w, c4b, wri, br, bi, lam, c3w, wout, *, layer, seq_len,
                permute_in):
    n, d = x2d.shape
    w_in_cols = win.shape[-1]
    f32 = jnp.float32
    groups = MIXER_GROUPS
    cur, nxt = _row_specs(n, d, SUB, groups)
    consts = (gmix, win, c4w, c4b, wri, br, bi, lam, c3w, wout)
    stacked = (win, wri, wout)
    return pl.pallas_call(
        functools.partial(_mixer_kernel, groups=groups, groups_per_seq=seq_len // SUB, d=d,
                          permute_in=permute_in),
        grid=(n // (groups * SUB),),
        in_specs=[cur, nxt] + [
            _resident_layer(c.shape, layer) if any(c is s for s in stacked)
            else _resident(c.shape) for c in consts],
        out_specs=cur,
        out_shape=jax.ShapeDtypeStruct((n, d), f32),
        scratch_shapes=[
            pltpu.VMEM((SUB, d), jnp.bfloat16),
            pltpu.VMEM((SUB, d), jnp.bfloat16),
            pltpu.VMEM((SUB, w_in_cols), f32),
            pltpu.VMEM((SUB, w_in_cols), f32),
            pltpu.VMEM((SUB, d), f32),
            pltpu.VMEM((SUB, d), f32),
            pltpu.VMEM((SUB, d), f32),
            pltpu.VMEM((SUB, d), f32),
            pltpu.VMEM((2, SUB, 2 * d), f32),
            pltpu.VMEM((SUB, d), f32),
            pltpu.VMEM((SUB, d), f32),
            pltpu.VMEM((SUBLANES, d), f32),
            pltpu.VMEM((SUBLANES, d), f32),
            pltpu.VMEM((SUB, d), jnp.bfloat16),
            pltpu.VMEM((_H4 * SUBLANES, d), f32),
            pltpu.VMEM((_H3 * SUBLANES, d), f32),
            pltpu.VMEM((SUBLANES, d), f32),
            pltpu.VMEM((SUBLANES, d), f32),
            pltpu.VMEM((SUB, d), f32),
            pltpu.VMEM((SUB, d), f32),
        ],
        compiler_params=pltpu.CompilerParams(
            dimension_semantics=("arbitrary",), vmem_limit_bytes=VMEM_LIMIT_BYTES),
        name="mixer",
    )(x2d, x2d, *consts)


def _ffn_call(x2d, p3d, gffn, wgu, wd, gple, wpg, wple, gfin, *, layer, final):
    n, d = x2d.shape
    d_ff = wgu.shape[-1] // 2
    groups, sub = FFN_GROUPS, FFN_SUB
    cur, nxt = _row_specs(n, d, sub, groups)
    p_spec = pl.BlockSpec((None, groups * sub, p3d.shape[-1]), lambda g: (layer, g, 0))
    consts = (gffn, wgu, wd, gple, wpg, wple, gfin)
    stacked = (wgu, wd, wpg, wple)
    return pl.pallas_call(
        functools.partial(_ffn_kernel, groups=groups, d_ff=d_ff, final=final),
        grid=(n // (groups * sub),),
        in_specs=[cur, nxt, p_spec] + [
            _resident_layer(c.shape, layer) if any(c is s for s in stacked)
            else _resident(c.shape) for c in consts],
        out_specs=cur,
        out_shape=jax.ShapeDtypeStruct((n, d), jnp.float32),
        scratch_shapes=[pltpu.VMEM((sub, d_ff), jnp.bfloat16),
                        pltpu.VMEM((sub, d_ff), jnp.bfloat16)],
        compiler_params=pltpu.CompilerParams(
            dimension_semantics=("arbitrary",), vmem_limit_bytes=VMEM_LIMIT_BYTES),
        name="ffn_ple",
    )(x2d, x2d, p3d, *consts)


def kernel(x, p, g_mix, w_in, conv4_w, conv4_b, w_rg_r, b_rg_r, w_rg_i, b_rg_i, lru_lambda, conv3_w, w_out, g_ffn, w_gate_up, w_down, g_ple, w_ple_gate, w_ple, g_final):
    bsz, seq_len, d = x.shape
    depth = p.shape[0]
    for groups, sub in ((MIXER_GROUPS, SUB), (FFN_GROUPS, FFN_SUB)):
        assert groups % 2 == 0 and seq_len % (groups * sub) == 0
    assert d % (RG_HEADS * LANES) == 0 and FFN_SUB == SUB
    row = lambda v: v.reshape(1, -1)
    rows8 = lambda v: jnp.repeat(v.reshape(-1, v.shape[-1]), SUBLANES, axis=0)

    xs = x.reshape(bsz * seq_len, d)
    win_p, wout_p = _pack_weights(w_in), _pack_weights(w_out)
    w_ri = jnp.concatenate([w_rg_r, w_rg_i], axis=-1)
    wri_p = _pack_weights(w_ri.reshape((-1,) + w_ri.shape[-2:])).reshape(
        w_ri.shape[:2] + (w_ri.shape[2] // 2, w_ri.shape[3]))
    wgu_p = _pack_weights(w_gate_up, interleave_halves=GATE_UP_BLOCK)
    wd_p = _pack_weights(w_down)
    wpg_p, wple_p = _pack_weights(w_ple_gate), _pack_weights(w_ple)

    p3d = p.reshape(depth, bsz * seq_len, -1)

    for i in range(depth):
        xs = _mixer_call(
            xs, row(g_mix[i]), win_p, rows8(conv4_w[i]), rows8(conv4_b[i]),
            wri_p, rows8(b_rg_r[i].reshape(-1)), rows8(b_rg_i[i].reshape(-1)), rows8(lru_lambda[i]), rows8(conv3_w[i]),
            wout_p, layer=i, seq_len=seq_len, permute_in=(i == 0))
        xs = _ffn_call(
            xs, p3d, row(g_ffn[i]), wgu_p, wd_p,
            row(g_ple[i]), wpg_p, wple_p,
            row(g_final), layer=i, final=(i == depth - 1))
    return xs.reshape(bsz, seq_len, d)
```

```python
import functools

import jax
import jax.numpy as jnp
from jax import lax
from jax.experimental import pallas as pl
from jax.experimental.pallas import tpu as pltpu

EPS = 1e-6
RG_C = 8.0
RG_HEADS = 4
CONV4_WIDTH = 4
CONV3_WIDTH = 3

SUBLANES = 8
LANES = 128
PACK_BLOCK_ROWS = 512
PACK_BLOCK_COLS = 2048
SUB = 128
J = SUB // SUBLANES
MIXER_GROUPS = 2
FFN_GROUPS = 8
FFN_SUB = 128
ROW_CHUNK = 2 * SUBLANES
GATE_UP_BLOCK = 256
VMEM_LIMIT_BYTES = 60000 * 1024

_H3 = CONV3_WIDTH - 1
_H4 = CONV4_WIDTH - 1


def _to_stream_order(a):
    return jnp.swapaxes(a.reshape(SUBLANES, J, a.shape[-1]), 0, 1).reshape(a.shape)


def _to_time_order(a):
    return jnp.swapaxes(a.reshape(J, SUBLANES, a.shape[-1]), 0, 1).reshape(a.shape)


def _rmsnorm(xs, g):
    ms = jnp.mean(xs * xs, axis=-1, keepdims=True)
    return (xs * lax.rsqrt(ms + EPS)) * g


def _sigmoid(z):
    return 0.5 * jnp.tanh(0.5 * z) + 0.5


def _softplus(z):
    return jnp.maximum(z, 0.0) + jnp.log1p(jnp.exp(-jnp.abs(z)))


def _rows(j, n=1):
    return slice(j * SUBLANES, (j + n) * SUBLANES)


def _largest_block(size, unit, cap):
    return max(b for b in range(unit, min(size, cap) + 1, unit) if size % b == 0)


def _pack(w):
    return pltpu.bitcast(w.astype(jnp.bfloat16), jnp.uint32)


def _pack_kernel(w_ref, o_ref):
    o_ref[...] = _pack(w_ref[...])


def _pack_interleaved_kernel(a_ref, b_ref, o_ref, *, block):
    for t in range(a_ref.shape[-1] // block):
        src = slice(t * block, (t + 1) * block)
        o_ref[:, 2 * t * block:(2 * t + 1) * block] = _pack(a_ref[:, src])
        o_ref[:, (2 * t + 1) * block:(2 * t + 2) * block] = _pack(b_ref[:, src])


def _pack_weights(w, interleave_halves=None):
    layers, k, n = w.shape
    bk = _largest_block(k, 2 * SUBLANES, PACK_BLOCK_ROWS)
    out_shape = jax.ShapeDtypeStruct((layers, k // 2, n), jnp.uint32)
    if interleave_halves is None:
        bn = _largest_block(n, LANES, PACK_BLOCK_COLS)
        return pl.pallas_call(
            _pack_kernel,
            grid=(layers, k // bk, n // bn),
            in_specs=[pl.BlockSpec((None, bk, bn), lambda l, i, j: (l, i, j))],
            out_specs=pl.BlockSpec((None, bk // 2, bn), lambda l, i, j: (l, i, j)),
            out_shape=out_shape,
            name="pack_weights",
        )(w)
    bk = _largest_block(k, 2 * SUBLANES, PACK_BLOCK_ROWS // 2)
    half = n // 2
    return pl.pallas_call(
        functools.partial(_pack_interleaved_kernel, block=interleave_halves),
        grid=(layers, k // bk),
        in_specs=[pl.BlockSpec((None, bk, half), lambda l, i: (l, i, 0)),
                  pl.BlockSpec((None, bk, half), lambda l, i: (l, i, 1))],
        out_specs=pl.BlockSpec((None, bk // 2, n), lambda l, i: (l, i, 0)),
        out_shape=out_shape,
        name="pack_weights_interleaved",
    )(w, w)


def _weights(packed):
    return pltpu.bitcast(packed, jnp.bfloat16)


def _mixer_kernel(x_ref, xn_ref, gmix_ref, win_ref, c4w_ref, c4b_ref, wri_ref, br_ref,
                  bi_ref, lam_ref, c3w_ref, wout_ref, o_ref,
                  hn0_ref, hn1_ref, z0_ref, z1_ref, xc0_ref, xc1_ref, yc0_ref, yc1_ref,
                  pre_ref, a_ref, u_ref, hl_ref, pl_ref, mrg_ref,
                  p4_ref, pq_ref, hc_ref, c_ref, xp0_ref, xp1_ref, *,
                  groups, groups_per_seq, d, permute_in):
    g = pl.program_id(0)
    hb = d // RG_HEADS
    o_rx, o_ry, o_cb, o_cc, o_cx, o_gr, o_gc = (k * d for k in range(7))
    first_stream = lax.broadcasted_iota(jnp.int32, (SUBLANES, hb), 0) == 0
    heads = [slice(h * hb, (h + 1) * hb) for h in range(RG_HEADS)]
    chunks = [(o_rx, o_ry), (o_cc, o_gr), (o_ry, o_cc), (o_gr, o_gc + d)]

    def zcols(off, cols):
        return slice(off + cols.start, off + cols.stop)

    def from_prev_stream(cur, prev):
        return jnp.where(first_stream, pltpu.roll(prev, 1, 0), pltpu.roll(cur, 1, 0))

    def normalize(xs, hn_ref):
        hn_ref[...] = _rmsnorm(xs, gmix_ref[...]).astype(jnp.bfloat16)

    def project(hn_ref, z_ref, chunk):
        lo, hi = chunks[chunk]
        z_ref[:, lo:hi] = jnp.dot(hn_ref[...], _weights(win_ref[:, lo:hi]),
                                  preferred_element_type=jnp.float32)

    def convs(z_ref, xc_ref, yc_ref, cols, seq_start):
        def history(h_ref, jj):
            return jnp.where(seq_start, 0.0, h_ref[_rows(jj), cols])

        c_rx = zcols(o_rx, cols)
        halo4 = [from_prev_stream(z_ref[_rows(J - _H4 + jj), c_rx], history(p4_ref, jj))
                 for jj in range(_H4)]
        p4_ref[:, cols] = z_ref[_rows(J - _H4, _H4), c_rx]

        def rx(j):
            return halo4[j + _H4] if j < 0 else z_ref[_rows(j), c_rx]

        for j in range(J):
            acc = c4b_ref[:, cols] + c4w_ref[_rows(0), cols] * rx(j - _H4)
            for k in range(1, CONV4_WIDTH):
                acc = acc + c4w_ref[_rows(k), cols] * rx(j - _H4 + k)
            xc_ref[_rows(j), cols] = acc

        def q(j):
            return z_ref[_rows(j), zcols(o_cc, cols)] * z_ref[_rows(j), zcols(o_cx, cols)]

        tail = [q(J - _H3 + jj) for jj in range(_H3)]
        window = [from_prev_stream(tail[jj], history(pq_ref, jj)) for jj in range(_H3)]
        for jj in range(_H3):
            pq_ref[_rows(jj), cols] = tail[jj]
        for j in range(J):
            qj = q(j) if j < J - _H3 else tail[j - (J - _H3)]
            yc = c3w_ref[_rows(_H3), cols] * qj
            for k in range(_H3):
                yc = yc + c3w_ref[_rows(k), cols] * window[k]
            yc_ref[_rows(j), cols] = yc
            window = window[1:] + [qj]

    def gates(xc_ref, h, cols):
        xh = xc_ref[:, cols]
        xb = xh.astype(jnp.bfloat16)
        slot = lax.rem(g, 2)
        r_cols = slice(2 * cols.start, 2 * cols.start + hb)
        i_cols = slice(2 * cols.start + hb, 2 * cols.stop)
        pre_ref[slot, :, 2 * cols.start:2 * cols.stop] = jnp.dot(
            xb, _weights(wri_ref[h]), preferred_element_type=jnp.float32)
        half_scale = (-0.5 * RG_C) * _softplus(-lam_ref[:, cols])
        for j in range(J):
            t_r = jnp.tanh(0.5 * (pre_ref[slot, _rows(j), r_cols] + br_ref[:, cols]))
            gi = _sigmoid(pre_ref[slot, _rows(j), i_cols] + bi_ref[:, cols])
            log_a = half_scale * t_r + half_scale
            a = jnp.exp(log_a)
            w = -jnp.tanh(log_a) * (1.0 + a * a)
            mult = jnp.where(w > 0.0, w * lax.rsqrt(w), 0.0)
            a_ref[_rows(j), cols] = a
            u_ref[_rows(j), cols] = mult * (gi * xh[_rows(j), :])

    def scan_merge(z_ref, yc_ref, cols, seq_start):
        hloc = jnp.zeros((SUBLANES, hb), jnp.float32)
        prod = jnp.ones((SUBLANES, hb), jnp.float32)
        for j in range(J):
            a = a_ref[_rows(j), cols]
            hloc = a * hloc + u_ref[_rows(j), cols]
            prod = a * prod
        hl_ref[:, cols] = hloc
        pl_ref[:, cols] = prod

        cur = jnp.where(seq_start, 0.0, hc_ref[0:1, cols])
        for s in range(SUBLANES):
            c_ref[s:s + 1, cols] = cur
            cur = hl_ref[s:s + 1, cols] + pl_ref[s:s + 1, cols] * cur
        hc_ref[0:1, cols] = cur
        hstate = c_ref[:, cols]

        for j2 in range(J // 2):
            merged = []
            for j in (2 * j2, 2 * j2 + 1):
                hstate = a_ref[_rows(j), cols] * hstate + u_ref[_rows(j), cols]
                y_rnn = jax.nn.gelu(z_ref[_rows(j), zcols(o_ry, cols)]) * hstate
                y_conv = z_ref[_rows(j), zcols(o_cb, cols)] * yc_ref[_rows(j), cols]
                merged.append(_sigmoid(z_ref[_rows(j), zcols(o_gr, cols)]) * y_rnn
                              + _sigmoid(z_ref[_rows(j), zcols(o_gc, cols)]) * y_conv)
            mrg_ref[_rows(2 * j2, 2), cols] = jnp.concatenate(merged, axis=0).astype(jnp.bfloat16)

    def output(h, cols):
        w_rows = slice(cols.start // 2, cols.stop // 2)
        return jnp.dot(mrg_ref[:, cols], _weights(wout_ref[w_rows, :]),
                       preferred_element_type=jnp.float32)

    starts = [(groups * g + k) % groups_per_seq == 0 for k in range(groups + 1)]
    slots = ((hn0_ref, z0_ref, xc0_ref, yc0_ref), (hn1_ref, z1_ref, xc1_ref, yc1_ref))
    xp_slots = (xp0_ref, xp1_ref)

    def group_rows(k):
        xs = x_ref[k * SUB:(k + 1) * SUB, :] if k < groups else xn_ref[...]
        if not permute_in:
            return xs
        xp_ref = xp_slots[k % 2]
        xp_ref[...] = _to_stream_order(xs)
        return xp_ref[...]

    @pl.when(g == 0)
    def _():
        p4_ref[...] = jnp.zeros_like(p4_ref)
        pq_ref[...] = jnp.zeros_like(pq_ref)
        hc_ref[...] = jnp.zeros_like(hc_ref)
        normalize(group_rows(0), hn0_ref)
        for chunk in range(len(chunks)):
            project(hn0_ref, z0_ref, chunk)
        for cols in heads:
            convs(z0_ref, xc0_ref, yc0_ref, cols, True)

    for k in range(groups):
        rows = slice(k * SUB, (k + 1) * SUB)
        _, z_cur, xc_cur, yc_cur = slots[k % 2]
        hn_nxt, z_nxt, xc_nxt, yc_nxt = slots[(k + 1) % 2]
        normalize(group_rows(k + 1), hn_nxt)
        acc = xp_slots[k % 2][...] if permute_in else x_ref[rows, :]
        for h, cols in enumerate(heads):
            gates(xc_cur, h, cols)
            project(hn_nxt, z_nxt, h)
            scan_merge(z_cur, yc_cur, cols, starts[k])
            if h > 0:
                acc = acc + output(h - 1, heads[h - 1])
        o_ref[rows, :] = acc + output(RG_HEADS - 1, heads[-1])
        for cols in heads:
            convs(z_nxt, xc_nxt, yc_nxt, cols, starts[k + 1])


def _ffn_kernel(x_ref, xn_ref, p_ref, gffn_ref, wgu_ref, wd_ref, gple_ref, wpg_ref,
                wple_ref, gfin_ref, o_ref, act0_ref, act1_ref, hn_ref, x2_ref, hn2_ref, *,
                groups, d_ff, final):
    g = pl.program_id(0)
    row_chunks = [slice(c * ROW_CHUNK, (c + 1) * ROW_CHUNK) for c in range(FFN_SUB // ROW_CHUNK)]

    def shifted(rc, base):
        return slice(base + rc.start, base + rc.stop)

    def project(src_ref, base, act_ref):
        for rc in row_chunks:
            hn_ref[rc, :] = _rmsnorm(src_ref[shifted(rc, base), :],
                                     gffn_ref[...]).astype(jnp.bfloat16)
        gu = jnp.dot(hn_ref[...], _weights(wgu_ref[...]), preferred_element_type=jnp.float32)
        b = GATE_UP_BLOCK
        for t in range(d_ff // b):
            gate = gu[:, 2 * t * b:(2 * t + 1) * b]
            up = gu[:, (2 * t + 1) * b:(2 * t + 2) * b]
            act_ref[:, t * b:(t + 1) * b] = ((gate * _sigmoid(gate)) * up).astype(jnp.bfloat16)

    def down(act_ref, base):
        dn = jnp.dot(act_ref[...], _weights(wd_ref[...]), preferred_element_type=jnp.float32)
        for rc in row_chunks:
            x2 = x_ref[shifted(rc, base), :] + dn[rc, :]
            x2_ref[rc, :] = x2
            hn2_ref[rc, :] = _rmsnorm(x2, gple_ref[...]).astype(jnp.bfloat16)

    def embed(base):
        rows = slice(base, base + FFN_SUB)
        gate_pre = jnp.dot(hn2_ref[...], _weights(wpg_ref[...]), preferred_element_type=jnp.float32)
        pb = _to_stream_order(p_ref[rows, :]).astype(jnp.bfloat16)
        pe = jnp.dot(pb, _weights(wple_ref[...]), preferred_element_type=jnp.float32)
        for rc in row_chunks:
            x3 = x2_ref[rc, :] + _sigmoid(gate_pre[rc, :]) * pe[rc, :]
            if final:
                x2_ref[rc, :] = _rmsnorm(x3, gfin_ref[...])
            else:
                o_ref[shifted(rc, base), :] = x3
        if final:
            o_ref[rows, :] = _to_time_order(x2_ref[...])

    @pl.when(g == 0)
    def _():
        project(x_ref, 0, act0_ref)

    act_slots = (act0_ref, act1_ref)
    for k in range(groups):
        base = k * FFN_SUB
        down(act_slots[k % 2], base)
        if k + 1 < groups:
            project(x_ref, base + FFN_SUB, act_slots[(k + 1) % 2])
        else:
            project(xn_ref, 0, act_slots[(k + 1) % 2])
        embed(base)


def _resident(shape):
    zeros = (0,) * len(shape)
    return pl.BlockSpec(shape, lambda g: zeros, pipeline_mode=pl.Buffered(1))


def _resident_layer(stacked_shape, layer):
    index = (layer,) + (0,) * (len(stacked_shape) - 1)
    return pl.BlockSpec((None,) + tuple(stacked_shape[1:]), lambda g: index,
                        pipeline_mode=pl.Buffered(1))


def _row_specs(n, d, sub, groups):
    last_group = n // sub - 1
    cur = pl.BlockSpec((groups * sub, d), lambda g: (g, 0))
    nxt = pl.BlockSpec((sub, d), lambda g: (jnp.minimum(groups * (g + 1), last_group), 0))
    return cur, nxt


def _mixer_call(x2d, gmix, win, c4w, c4b, wri, br, bi, lam, c3w, wout, *, layer, seq_len,
                permute_in):
    n, d = x2d.shape
    w_in_cols = win.shape[-1]
    f32 = jnp.float32
    groups = MIXER_GROUPS
    cur, nxt = _row_specs(n, d, SUB, groups)
    consts = (gmix, win, c4w, c4b, wri, br, bi, lam, c3w, wout)
    stacked = (win, wri, wout)
    return pl.pallas_call(
        functools.partial(_mixer_kernel, groups=groups, groups_per_seq=seq_len // SUB, d=d,
                          permute_in=permute_in),
        grid=(n // (groups * SUB),),
        in_specs=[cur, nxt] + [
            _resident_layer(c.shape, layer) if any(c is s for s in stacked)
            else _resident(c.shape) for c in consts],
        out_specs=cur,
        out_shape=jax.ShapeDtypeStruct((n, d), f32),
        scratch_shapes=[
            pltpu.VMEM((SUB, d), jnp.bfloat16),
            pltpu.VMEM((SUB, d), jnp.bfloat16),
            pltpu.VMEM((SUB, w_in_cols), f32),
            pltpu.VMEM((SUB, w_in_cols), f32),
            pltpu.VMEM((SUB, d), f32),
            pltpu.VMEM((SUB, d), f32),
            pltpu.VMEM((SUB, d), f32),
            pltpu.VMEM((SUB, d), f32),
            pltpu.VMEM((2, SUB, 2 * d), f32),
            pltpu.VMEM((SUB, d), f32),
            pltpu.VMEM((SUB, d), f32),
            pltpu.VMEM((SUBLANES, d), f32),
            pltpu.VMEM((SUBLANES, d), f32),
            pltpu.VMEM((SUB, d), jnp.bfloat16),
            pltpu.VMEM((_H4 * SUBLANES, d), f32),
            pltpu.VMEM((_H3 * SUBLANES, d), f32),
            pltpu.VMEM((SUBLANES, d), f32),
            pltpu.VMEM((SUBLANES, d), f32),
            pltpu.VMEM((SUB, d), f32),
            pltpu.VMEM((SUB, d), f32),
        ],
        compiler_params=pltpu.CompilerParams(
            dimension_semantics=("arbitrary",), vmem_limit_bytes=VMEM_LIMIT_BYTES),
        name="mixer",
    )(x2d, x2d, *consts)


def _ffn_call(x2d, p3d, gffn, wgu, wd, gple, wpg, wple, gfin, *, layer, final):
    n, d = x2d.shape
    d_ff = wgu.shape[-1] // 2
    groups, sub = FFN_GROUPS, FFN_SUB
    cur, nxt = _row_specs(n, d, sub, groups)
    p_spec = pl.BlockSpec((None, groups * sub, p3d.shape[-1]), lambda g: (layer, g, 0))
    consts = (gffn, wgu, wd, gple, wpg, wple, gfin)
    stacked = (wgu, wd, wpg, wple)
    return pl.pallas_call(
        functools.partial(_ffn_kernel, groups=groups, d_ff=d_ff, final=final),
        grid=(n // (groups * sub),),
        in_specs=[cur, nxt, p_spec] + [
            _resident_layer(c.shape, layer) if any(c is s for s in stacked)
            else _resident(c.shape) for c in consts],
        out_specs=cur,
        out_shape=jax.ShapeDtypeStruct((n, d), jnp.float32),
        scratch_shapes=[pltpu.VMEM((sub, d_ff), jnp.bfloat16),
                        pltpu.VMEM((sub, d_ff), jnp.bfloat16),
                        pltpu.VMEM((sub, d), jnp.bfloat16),
                        pltpu.VMEM((sub, d), jnp.float32),
                        pltpu.VMEM((sub, d), jnp.bfloat16)],
        compiler_params=pltpu.CompilerParams(
            dimension_semantics=("arbitrary",), vmem_limit_bytes=VMEM_LIMIT_BYTES),
        name="ffn_ple",
    )(x2d, x2d, p3d, *consts)


def kernel(x, p, g_mix, w_in, conv4_w, conv4_b, w_rg_r, b_rg_r, w_rg_i, b_rg_i, lru_lambda, conv3_w, w_out, g_ffn, w_gate_up, w_down, g_ple, w_ple_gate, w_ple, g_final):
    bsz, seq_len, d = x.shape
    depth = p.shape[0]
    for groups, sub in ((MIXER_GROUPS, SUB), (FFN_GROUPS, FFN_SUB)):
        assert groups % 2 == 0 and seq_len % (groups * sub) == 0
    assert d % (RG_HEADS * LANES) == 0 and FFN_SUB == SUB
    row = lambda v: v.reshape(1, -1)
    rows8 = lambda v: jnp.repeat(v.reshape(-1, v.shape[-1]), SUBLANES, axis=0)

    xs = x.reshape(bsz * seq_len, d)
    win_p, wout_p = _pack_weights(w_in), _pack_weights(w_out)
    w_ri = jnp.concatenate([w_rg_r, w_rg_i], axis=-1)
    wri_p = _pack_weights(w_ri.reshape((-1,) + w_ri.shape[-2:])).reshape(
        w_ri.shape[:2] + (w_ri.shape[2] // 2, w_ri.shape[3]))
    wgu_p = _pack_weights(w_gate_up, interleave_halves=GATE_UP_BLOCK)
    wd_p = _pack_weights(w_down)
    wpg_p, wple_p = _pack_weights(w_ple_gate), _pack_weights(w_ple)

    p3d = p.reshape(depth, bsz * seq_len, -1)

    for i in range(depth):
        xs = _mixer_call(
            xs, row(g_mix[i]), win_p, rows8(conv4_w[i]), rows8(conv4_b[i]),
            wri_p, rows8(b_rg_r[i].reshape(-1)), rows8(b_rg_i[i].reshape(-1)), rows8(lru_lambda[i]), rows8(conv3_w[i]),
            wout_p, layer=i, seq_len=seq_len, permute_in=(i == 0))
        xs = _ffn_call(
            xs, p3d, row(g_ffn[i]), wgu_p, wd_p,
            row(g_ple[i]), wpg_p, wple_p,
            row(g_final), layer=i, final=(i == depth - 1))
    return xs.reshape(bsz, seq_len, d)
```

```python
import functools

import jax
import jax.numpy as jnp
from jax import lax
from jax.experimental import pallas as pl
from jax.experimental.pallas import tpu as pltpu

EPS = 1e-6
RG_C = 8.0
RG_HEADS = 4
CONV4_WIDTH = 4
CONV3_WIDTH = 3

SUBLANES = 8
LANES = 128
PACK_BLOCK_ROWS = 512
PACK_BLOCK_COLS = 2048
SUB = 128
J = SUB // SUBLANES
MIXER_GROUPS = 2
FFN_GROUPS = 8
FFN_SUB = 128
ROW_CHUNK = 2 * SUBLANES
GATE_UP_BLOCK = 256
VMEM_LIMIT_BYTES = 60000 * 1024

_H3 = CONV3_WIDTH - 1
_H4 = CONV4_WIDTH - 1


def _to_stream_order(a):
    return jnp.swapaxes(a.reshape(SUBLANES, J, a.shape[-1]), 0, 1).reshape(a.shape)


def _to_time_order(a):
    return jnp.swapaxes(a.reshape(J, SUBLANES, a.shape[-1]), 0, 1).reshape(a.shape)


def _rmsnorm(xs, g):
    ms = jnp.mean(xs * xs, axis=-1, keepdims=True)
    return (xs * lax.rsqrt(ms + EPS)) * g


def _sigmoid(z):
    return 0.5 * jnp.tanh(0.5 * z) + 0.5


def _softplus(z):
    return jnp.maximum(z, 0.0) + jnp.log1p(jnp.exp(-jnp.abs(z)))


def _rows(j, n=1):
    return slice(j * SUBLANES, (j + n) * SUBLANES)


def _largest_block(size, unit, cap):
    return max(b for b in range(unit, min(size, cap) + 1, unit) if size % b == 0)


def _pack(w):
    return pltpu.bitcast(w.astype(jnp.bfloat16), jnp.uint32)


def _pack_kernel(w_ref, o_ref):
    o_ref[...] = _pack(w_ref[...])


def _pack_interleaved_kernel(a_ref, b_ref, o_ref, *, block):
    for t in range(a_ref.shape[-1] // block):
        src = slice(t * block, (t + 1) * block)
        o_ref[:, 2 * t * block:(2 * t + 1) * block] = _pack(a_ref[:, src])
        o_ref[:, (2 * t + 1) * block:(2 * t + 2) * block] = _pack(b_ref[:, src])


def _pack_weights(w, interleave_halves=None):
    layers, k, n = w.shape
    bk = _largest_block(k, 2 * SUBLANES, PACK_BLOCK_ROWS)
    out_shape = jax.ShapeDtypeStruct((layers, k // 2, n), jnp.uint32)
    if interleave_halves is None:
        bn = _largest_block(n, LANES, PACK_BLOCK_COLS)
        return pl.pallas_call(
            _pack_kernel,
            grid=(layers, k // bk, n // bn),
            in_specs=[pl.BlockSpec((None, bk, bn), lambda l, i, j: (l, i, j))],
            out_specs=pl.BlockSpec((None, bk // 2, bn), lambda l, i, j: (l, i, j)),
            out_shape=out_shape,
            name="pack_weights",
        )(w)
    bk = _largest_block(k, 2 * SUBLANES, PACK_BLOCK_ROWS // 2)
    half = n // 2
    return pl.pallas_call(
        functools.partial(_pack_interleaved_kernel, block=interleave_halves),
        grid=(layers, k // bk),
        in_specs=[pl.BlockSpec((None, bk, half), lambda l, i: (l, i, 0)),
                  pl.BlockSpec((None, bk, half), lambda l, i: (l, i, 1))],
        out_specs=pl.BlockSpec((None, bk // 2, n), lambda l, i: (l, i, 0)),
        out_shape=out_shape,
        name="pack_weights_interleaved",
    )(w, w)


def _weights(packed):
    return pltpu.bitcast(packed, jnp.bfloat16)


def _mixer_kernel(x_ref, xn_ref, gmix_ref, win_ref, c4w_ref, c4b_ref, wri_ref, br_ref,
                  bi_ref, lam_ref, c3w_ref, wout_ref, o_ref,
                  hn0_ref, hn1_ref, z0_ref, z1_ref, xc0_ref, xc1_ref, yc0_ref, yc1_ref,
                  pre_ref, a_ref, u_ref, hl_ref, pl_ref, mrg_ref,
                  p4_ref, pq_ref, hc_ref, c_ref, xp0_ref, xp1_ref, *,
                  groups, groups_per_seq, d, permute_in):
    g = pl.program_id(0)
    hb = d // RG_HEADS
    o_rx, o_ry, o_cb, o_cc, o_cx, o_gr, o_gc = (k * d for k in range(7))
    first_stream = lax.broadcasted_iota(jnp.int32, (SUBLANES, hb), 0) == 0
    heads = [slice(h * hb, (h + 1) * hb) for h in range(RG_HEADS)]
    chunks = [(o_rx, o_ry), (o_cc, o_gr), (o_ry, o_cc), (o_gr, o_gc + d)]

    def zcols(off, cols):
        return slice(off + cols.start, off + cols.stop)

    def from_prev_stream(cur, prev):
        return jnp.where(first_stream, pltpu.roll(prev, 1, 0), pltpu.roll(cur, 1, 0))

    def normalize(src, hn_ref):
        src_ref, base = src
        for c in range(SUB // ROW_CHUNK):
            lo = c * ROW_CHUNK
            hn_ref[lo:lo + ROW_CHUNK, :] = _rmsnorm(
                src_ref[base + lo:base + lo + ROW_CHUNK, :], gmix_ref[...]).astype(jnp.bfloat16)

    def project(hn_ref, z_ref, chunk):
        lo, hi = chunks[chunk]
        z_ref[:, lo:hi] = jnp.dot(hn_ref[...], _weights(win_ref[:, lo:hi]),
                                  preferred_element_type=jnp.float32)

    def convs(z_ref, xc_ref, yc_ref, cols, seq_start):
        def history(h_ref, jj):
            return jnp.where(seq_start, 0.0, h_ref[_rows(jj), cols])

        c_rx = zcols(o_rx, cols)
        halo4 = [from_prev_stream(z_ref[_rows(J - _H4 + jj), c_rx], history(p4_ref, jj))
                 for jj in range(_H4)]
        p4_ref[:, cols] = z_ref[_rows(J - _H4, _H4), c_rx]

        def rx(j):
            return halo4[j + _H4] if j < 0 else z_ref[_rows(j), c_rx]

        for j in range(J):
            acc = c4b_ref[:, cols] + c4w_ref[_rows(0), cols] * rx(j - _H4)
            for k in range(1, CONV4_WIDTH):
                acc = acc + c4w_ref[_rows(k), cols] * rx(j - _H4 + k)
            xc_ref[_rows(j), cols] = acc

        def q(j):
            return z_ref[_rows(j), zcols(o_cc, cols)] * z_ref[_rows(j), zcols(o_cx, cols)]

        tail = [q(J - _H3 + jj) for jj in range(_H3)]
        window = [from_prev_stream(tail[jj], history(pq_ref, jj)) for jj in range(_H3)]
        for jj in range(_H3):
            pq_ref[_rows(jj), cols] = tail[jj]
        for j in range(J):
            qj = q(j) if j < J - _H3 else tail[j - (J - _H3)]
            yc = c3w_ref[_rows(_H3), cols] * qj
            for k in range(_H3):
                yc = yc + c3w_ref[_rows(k), cols] * window[k]
            yc_ref[_rows(j), cols] = yc
            window = window[1:] + [qj]

    def gates(xc_ref, h, cols):
        xh = xc_ref[:, cols]
        xb = xh.astype(jnp.bfloat16)
        slot = lax.rem(g, 2)
        r_cols = slice(2 * cols.start, 2 * cols.start + hb)
        i_cols = slice(2 * cols.start + hb, 2 * cols.stop)
        pre_ref[slot, :, 2 * cols.start:2 * cols.stop] = jnp.dot(
            xb, _weights(wri_ref[h]), preferred_element_type=jnp.float32)
        half_scale = (-0.5 * RG_C) * _softplus(-lam_ref[:, cols])
        for j in range(J):
            t_r = jnp.tanh(0.5 * (pre_ref[slot, _rows(j), r_cols] + br_ref[:, cols]))
            gi = _sigmoid(pre_ref[slot, _rows(j), i_cols] + bi_ref[:, cols])
            log_a = half_scale * t_r + half_scale
            a = jnp.exp(log_a)
            w = -jnp.tanh(log_a) * (1.0 + a * a)
            mult = jnp.where(w > 0.0, w * lax.rsqrt(w), 0.0)
            a_ref[_rows(j), cols] = a
            u_ref[_rows(j), cols] = mult * (gi * xh[_rows(j), :])

    def scan_merge(z_ref, yc_ref, cols, seq_start):
        hloc = jnp.zeros((SUBLANES, hb), jnp.float32)
        prod = jnp.ones((SUBLANES, hb), jnp.float32)
        for j in range(J):
            a = a_ref[_rows(j), cols]
            hloc = a * hloc + u_ref[_rows(j), cols]
            prod = a * prod
        hl_ref[:, cols] = hloc
        pl_ref[:, cols] = prod

        cur = jnp.where(seq_start, 0.0, hc_ref[0:1, cols])
        for s in range(SUBLANES):
            c_ref[s:s + 1, cols] = cur
            cur = hl_ref[s:s + 1, cols] + pl_ref[s:s + 1, cols] * cur
        hc_ref[0:1, cols] = cur
        hstate = c_ref[:, cols]

        for j2 in range(J // 2):
            merged = []
            for j in (2 * j2, 2 * j2 + 1):
                hstate = a_ref[_rows(j), cols] * hstate + u_ref[_rows(j), cols]
                y_rnn = jax.nn.gelu(z_ref[_rows(j), zcols(o_ry, cols)]) * hstate
                y_conv = z_ref[_rows(j), zcols(o_cb, cols)] * yc_ref[_rows(j), cols]
                merged.append(_sigmoid(z_ref[_rows(j), zcols(o_gr, cols)]) * y_rnn
                              + _sigmoid(z_ref[_rows(j), zcols(o_gc, cols)]) * y_conv)
            mrg_ref[_rows(2 * j2, 2), cols] = jnp.concatenate(merged, axis=0).astype(jnp.bfloat16)

    def output(h, cols):
        w_rows = slice(cols.start // 2, cols.stop // 2)
        return jnp.dot(mrg_ref[:, cols], _weights(wout_ref[w_rows, :]),
                       preferred_element_type=jnp.float32)

    starts = [(groups * g + k) % groups_per_seq == 0 for k in range(groups + 1)]
    slots = ((hn0_ref, z0_ref, xc0_ref, yc0_ref), (hn1_ref, z1_ref, xc1_ref, yc1_ref))
    xp_slots = (xp0_ref, xp1_ref)

    def group_rows(k):
        src_ref, base = (x_ref, k * SUB) if k < groups else (xn_ref, 0)
        if not permute_in:
            return src_ref, base
        xp_ref = xp_slots[k % 2]
        xp_ref[...] = _to_stream_order(src_ref[base:base + SUB, :])
        return xp_ref, 0

    @pl.when(g == 0)
    def _():
        p4_ref[...] = jnp.zeros_like(p4_ref)
        pq_ref[...] = jnp.zeros_like(pq_ref)
        hc_ref[...] = jnp.zeros_like(hc_ref)
        normalize(group_rows(0), hn0_ref)
        for chunk in range(len(chunks)):
            project(hn0_ref, z0_ref, chunk)
        for cols in heads:
            convs(z0_ref, xc0_ref, yc0_ref, cols, True)

    for k in range(groups):
        rows = slice(k * SUB, (k + 1) * SUB)
        _, z_cur, xc_cur, yc_cur = slots[k % 2]
        hn_nxt, z_nxt, xc_nxt, yc_nxt = slots[(k + 1) % 2]
        normalize(group_rows(k + 1), hn_nxt)
        acc = xp_slots[k % 2][...] if permute_in else x_ref[rows, :]
        for h, cols in enumerate(heads):
            gates(xc_cur, h, cols)
            project(hn_nxt, z_nxt, h)
            scan_merge(z_cur, yc_cur, cols, starts[k])
            if h > 0:
                acc = acc + output(h - 1, heads[h - 1])
        o_ref[rows, :] = acc + output(RG_HEADS - 1, heads[-1])
        for cols in heads:
            convs(z_nxt, xc_nxt, yc_nxt, cols, starts[k + 1])


def _ffn_kernel(x_ref, xn_ref, p_ref, gffn_ref, wgu_ref, wd_ref, gple_ref, wpg_ref,
                wple_ref, gfin_ref, o_ref, act0_ref, act1_ref, hn_ref, x2_ref, hn2_ref, *,
                groups, d_ff, final):
    g = pl.program_id(0)
    row_chunks = [slice(c * ROW_CHUNK, (c + 1) * ROW_CHUNK) for c in range(FFN_SUB // ROW_CHUNK)]

    def shifted(rc, base):
        return slice(base + rc.start, base + rc.stop)

    def project(src_ref, base, act_ref):
        for rc in row_chunks:
            hn_ref[rc, :] = _rmsnorm(src_ref[shifted(rc, base), :],
                                     gffn_ref[...]).astype(jnp.bfloat16)
        gu = jnp.dot(hn_ref[...], _weights(wgu_ref[...]), preferred_element_type=jnp.float32)
        b = GATE_UP_BLOCK
        for t in range(d_ff // b):
            gate = gu[:, 2 * t * b:(2 * t + 1) * b]
            up = gu[:, (2 * t + 1) * b:(2 * t + 2) * b]
            act_ref[:, t * b:(t + 1) * b] = ((gate * _sigmoid(gate)) * up).astype(jnp.bfloat16)

    def down(act_ref, base):
        dn = jnp.dot(act_ref[...], _weights(wd_ref[...]), preferred_element_type=jnp.float32)
        for rc in row_chunks:
            x2 = x_ref[shifted(rc, base), :] + dn[rc, :]
            x2_ref[rc, :] = x2
            hn2_ref[rc, :] = _rmsnorm(x2, gple_ref[...]).astype(jnp.bfloat16)

    def embed(base):
        rows = slice(base, base + FFN_SUB)
        gate_pre = jnp.dot(hn2_ref[...], _weights(wpg_ref[...]), preferred_element_type=jnp.float32)
        pb = _to_stream_order(p_ref[rows, :]).astype(jnp.bfloat16)
        pe = jnp.dot(pb, _weights(wple_ref[...]), preferred_element_type=jnp.float32)
        for rc in row_chunks:
            x3 = x2_ref[rc, :] + _sigmoid(gate_pre[rc, :]) * pe[rc, :]
            if final:
                x2_ref[rc, :] = _rmsnorm(x3, gfin_ref[...])
            else:
                o_ref[shifted(rc, base), :] = x3
        if final:
            o_ref[rows, :] = _to_time_order(x2_ref[...])

    @pl.when(g == 0)
    def _():
        project(x_ref, 0, act0_ref)

    act_slots = (act0_ref, act1_ref)
    for k in range(groups):
        base = k * FFN_SUB
        down(act_slots[k % 2], base)
        if k + 1 < groups:
            project(x_ref, base + FFN_SUB, act_slots[(k + 1) % 2])
        else:
            project(xn_ref, 0, act_slots[(k + 1) % 2])
        embed(base)


def _resident(shape):
    zeros = (0,) * len(shape)
    return pl.BlockSpec(shape, lambda g: zeros, pipeline_mode=pl.Buffered(1))


def _resident_layer(stacked_shape, layer):
    index = (layer,) + (0,) * (len(stacked_shape) - 1)
    return pl.BlockSpec((None,) + tuple(stacked_shape[1:]), lambda g: index,
                        pipeline_mode=pl.Buffered(1))


def _row_specs(n, d, sub, groups):
    last_group = n // sub - 1
    cur = pl.BlockSpec((groups * sub, d), lambda g: (g, 0))
    nxt = pl.BlockSpec((sub, d), lambda g: (jnp.minimum(groups * (g + 1), last_group), 0))
    return cur, nxt


def _mixer_call(x2d, gmix, win, c4w, c4b, wri, br, bi, lam, c3w, wout, *, layer, seq_len,
                permute_in):
    n, d = x2d.shape
    w_in_cols = win.shape[-1]
    f32 = jnp.float32
    groups = MIXER_GROUPS
    cur, nxt = _row_specs(n, d, SUB, groups)
    consts = (gmix, win, c4w, c4b, wri, br, bi, lam, c3w, wout)
    stacked = (win, wri, wout)
    return pl.pallas_call(
        functools.partial(_mixer_kernel, groups=groups, groups_per_seq=seq_len // SUB, d=d,
                          permute_in=permute_in),
        grid=(n // (groups * SUB),),
        in_specs=[cur, nxt] + [
            _resident_layer(c.shape, layer) if any(c is s for s in stacked)
            else _resident(c.shape) for c in consts],
        out_specs=cur,
        out_shape=jax.ShapeDtypeStruct((n, d), f32),
        scratch_shapes=[
            pltpu.VMEM((SUB, d), jnp.bfloat16),
            pltpu.VMEM((SUB, d), jnp.bfloat16),
            pltpu.VMEM((SUB, w_in_cols), f32),
            pltpu.VMEM((SUB, w_in_cols), f32),
            pltpu.VMEM((SUB, d), f32),
            pltpu.VMEM((SUB, d), f32),
            pltpu.VMEM((SUB, d), f32),
            pltpu.VMEM((SUB, d), f32),
            pltpu.VMEM((2, SUB, 2 * d), f32),
            pltpu.VMEM((SUB, d), f32),
            pltpu.VMEM((SUB, d), f32),
            pltpu.VMEM((SUBLANES, d), f32),
            pltpu.VMEM((SUBLANES, d), f32),
            pltpu.VMEM((SUB, d), jnp.bfloat16),
            pltpu.VMEM((_H4 * SUBLANES, d), f32),
            pltpu.VMEM((_H3 * SUBLANES, d), f32),
            pltpu.VMEM((SUBLANES, d), f32),
            pltpu.VMEM((SUBLANES, d), f32),
            pltpu.VMEM((SUB, d), f32),
            pltpu.VMEM((SUB, d), f32),
        ],
        compiler_params=pltpu.CompilerParams(
            dimension_semantics=("arbitrary",), vmem_limit_bytes=VMEM_LIMIT_BYTES),
        name="mixer",
    )(x2d, x2d, *consts)


def _ffn_call(x2d, p3d, gffn, wgu, wd, gple, wpg, wple, gfin, *, layer, final):
    n, d = x2d.shape
    d_ff = wgu.shape[-1] // 2
    groups, sub = FFN_GROUPS, FFN_SUB
    cur, nxt = _row_specs(n, d, sub, groups)
    p_spec = pl.BlockSpec((None, groups * sub, p3d.shape[-1]), lambda g: (layer, g, 0))
    consts = (gffn, wgu, wd, gple, wpg, wple, gfin)
    stacked = (wgu, wd, wpg, wple)
    return pl.pallas_call(
        functools.partial(_ffn_kernel, groups=groups, d_ff=d_ff, final=final),
        grid=(n // (groups * sub),),
        in_specs=[cur, nxt, p_spec] + [
            _resident_layer(c.shape, layer) if any(c is s for s in stacked)
            else _resident(c.shape) for c in consts],
        out_specs=cur,
        out_shape=jax.ShapeDtypeStruct((n, d), jnp.float32),
        scratch_shapes=[pltpu.VMEM((sub, d_ff), jnp.bfloat16),
                        pltpu.VMEM((sub, d_ff), jnp.bfloat16),
                        pltpu.VMEM((sub, d), jnp.bfloat16),
                        pltpu.VMEM((sub, d), jnp.float32),
                        pltpu.VMEM((sub, d), jnp.bfloat16)],
        compiler_params=pltpu.CompilerParams(
            dimension_semantics=("arbitrary",), vmem_limit_bytes=VMEM_LIMIT_BYTES),
        name="ffn_ple",
    )(x2d, x2d, p3d, *consts)


def kernel(x, p, g_mix, w_in, conv4_w, conv4_b, w_rg_r, b_rg_r, w_rg_i, b_rg_i, lru_lambda, conv3_w, w_out, g_ffn, w_gate_up, w_down, g_ple, w_ple_gate, w_ple, g_final):
    bsz, seq_len, d = x.shape
    depth = p.shape[0]
    for groups, sub in ((MIXER_GROUPS, SUB), (FFN_GROUPS, FFN_SUB)):
        assert groups % 2 == 0 and seq_len % (groups * sub) == 0
    assert d % (RG_HEADS * LANES) == 0 and FFN_SUB == SUB
    row = lambda v: v.reshape(1, -1)
    rows8 = lambda v: jnp.repeat(v.reshape(-1, v.shape[-1]), SUBLANES, axis=0)

    xs = x.reshape(bsz * seq_len, d)
    win_p, wout_p = _pack_weights(w_in), _pack_weights(w_out)
    w_ri = jnp.concatenate([w_rg_r, w_rg_i], axis=-1)
    wri_p = _pack_weights(w_ri.reshape((-1,) + w_ri.shape[-2:])).reshape(
        w_ri.shape[:2] + (w_ri.shape[2] // 2, w_ri.shape[3]))
    wgu_p = _pack_weights(w_gate_up, interleave_halves=GATE_UP_BLOCK)
    wd_p = _pack_weights(w_down)
    wpg_p, wple_p = _pack_weights(w_ple_gate), _pack_weights(w_ple)

    p3d = p.reshape(depth, bsz * seq_len, -1)

    for i in range(depth):
        xs = _mixer_call(
            xs, row(g_mix[i]), win_p, rows8(conv4_w[i]), rows8(conv4_b[i]),
            wri_p, rows8(b_rg_r[i].reshape(-1)), rows8(b_rg_i[i].reshape(-1)), rows8(lru_lambda[i]), rows8(conv3_w[i]),
            wout_p, layer=i, seq_len=seq_len, permute_in=(i == 0))
        xs = _ffn_call(
            xs, p3d, row(g_ffn[i]), wgu_p, wd_p,
            row(g_ple[i]), wpg_p, wple_p,
            row(g_final), layer=i, final=(i == depth - 1))
    return xs.reshape(bsz, seq_len, d)
```

```python
import functools

import jax
import jax.numpy as jnp
from jax import lax
from jax.experimental import pallas as pl
from jax.experimental.pallas import tpu as pltpu

EPS = 1e-6
RG_C = 8.0
RG_HEADS = 4
CONV4_WIDTH = 4
CONV3_WIDTH = 3

SUBLANES = 8
LANES = 128
PACK_BLOCK_ROWS = 512
PACK_BLOCK_COLS = 2048
SUB = 128
J = SUB // SUBLANES
MIXER_GROUPS = 2
FFN_GROUPS = 8
FFN_SUB = 128
GATE_UP_BLOCK = 256
VMEM_LIMIT_BYTES = 60000 * 1024

_H3 = CONV3_WIDTH - 1
_H4 = CONV4_WIDTH - 1


def _to_stream_order(a):
    return jnp.swapaxes(a.reshape(SUBLANES, J, a.shape[-1]), 0, 1).reshape(a.shape)


def _to_time_order(a):
    return jnp.swapaxes(a.reshape(J, SUBLANES, a.shape[-1]), 0, 1).reshape(a.shape)


def _rmsnorm(xs, g):
    ms = jnp.mean(xs * xs, axis=-1, keepdims=True)
    return (xs * lax.rsqrt(ms + EPS)) * g


def _sigmoid(z):
    return 0.5 * jnp.tanh(0.5 * z) + 0.5


def _softplus(z):
    return jnp.maximum(z, 0.0) + jnp.log1p(jnp.exp(-jnp.abs(z)))


def _rows(j, n=1):
    return slice(j * SUBLANES, (j + n) * SUBLANES)


def _largest_block(size, unit, cap):
    return max(b for b in range(unit, min(size, cap) + 1, unit) if size % b == 0)


def _pack(w):
    return pltpu.bitcast(w.astype(jnp.bfloat16), jnp.uint32)


def _pack_kernel(w_ref, o_ref):
    o_ref[...] = _pack(w_ref[...])


def _pack_interleaved_kernel(a_ref, b_ref, o_ref, *, block):
    for t in range(a_ref.shape[-1] // block):
        src = slice(t * block, (t + 1) * block)
        o_ref[:, 2 * t * block:(2 * t + 1) * block] = _pack(a_ref[:, src])
        o_ref[:, (2 * t + 1) * block:(2 * t + 2) * block] = _pack(b_ref[:, src])


def _pack_weights(w, interleave_halves=None):
    layers, k, n = w.shape
    bk = _largest_block(k, 2 * SUBLANES, PACK_BLOCK_ROWS)
    out_shape = jax.ShapeDtypeStruct((layers, k // 2, n), jnp.uint32)
    if interleave_halves is None:
        bn = _largest_block(n, LANES, PACK_BLOCK_COLS)
        return pl.pallas_call(
            _pack_kernel,
            grid=(layers, k // bk, n // bn),
            in_specs=[pl.BlockSpec((None, bk, bn), lambda l, i, j: (l, i, j))],
            out_specs=pl.BlockSpec((None, bk // 2, bn), lambda l, i, j: (l, i, j)),
            out_shape=out_shape,
            name="pack_weights",
        )(w)
    bk = _largest_block(k, 2 * SUBLANES, PACK_BLOCK_ROWS // 2)
    half = n // 2
    return pl.pallas_call(
        functools.partial(_pack_interleaved_kernel, block=interleave_halves),
        grid=(layers, k // bk),
        in_specs=[pl.BlockSpec((None, bk, half), lambda l, i: (l, i, 0)),
                  pl.BlockSpec((None, bk, half), lambda l, i: (l, i, 1))],
        out_specs=pl.BlockSpec((None, bk // 2, n), lambda l, i: (l, i, 0)),
        out_shape=out_shape,
        name="pack_weights_interleaved",
    )(w, w)


def _weights(packed):
    return pltpu.bitcast(packed, jnp.bfloat16)


def _mixer_kernel(x_ref, xn_ref, gmix_ref, win_ref, c4w_ref, c4b_ref, wri_ref, br_ref,
                  bi_ref, lam_ref, c3w_ref, wout_ref, o_ref,
                  hn0_ref, hn1_ref, z0_ref, z1_ref, xc0_ref, xc1_ref, yc0_ref, yc1_ref,
                  pre_ref, a_ref, u_ref, hl_ref, pl_ref, mrg_ref,
                  p4_ref, pq_ref, hc_ref, c_ref, xp0_ref, xp1_ref, *,
                  groups, groups_per_seq, d, permute_in):
    g = pl.program_id(0)
    hb = d // RG_HEADS
    o_rx, o_ry, o_cb, o_cc, o_cx, o_gr, o_gc = (k * d for k in range(7))
    first_stream = lax.broadcasted_iota(jnp.int32, (SUBLANES, hb), 0) == 0
    heads = [slice(h * hb, (h + 1) * hb) for h in range(RG_HEADS)]
    chunks = [(o_rx, o_ry), (o_cc, o_gr), (o_ry, o_cc), (o_gr, o_gc + d)]

    def zcols(off, cols):
        return slice(off + cols.start, off + cols.stop)

    def from_prev_stream(cur, prev):
        return jnp.where(first_stream, pltpu.roll(prev, 1, 0), pltpu.roll(cur, 1, 0))

    def normalize(xs, hn_ref):
        hn_ref[...] = _rmsnorm(xs, gmix_ref[...]).astype(jnp.bfloat16)

    def project(hn_ref, z_ref, chunk):
        lo, hi = chunks[chunk]
        z_ref[:, lo:hi] = jnp.dot(hn_ref[...], _weights(win_ref[:, lo:hi]),
                                  preferred_element_type=jnp.float32)

    def convs(z_ref, xc_ref, yc_ref, cols, seq_start):
        def history(h_ref, jj):
            return jnp.where(seq_start, 0.0, h_ref[_rows(jj), cols])

        c_rx = zcols(o_rx, cols)
        halo4 = [from_prev_stream(z_ref[_rows(J - _H4 + jj), c_rx], history(p4_ref, jj))
                 for jj in range(_H4)]
        p4_ref[:, cols] = z_ref[_rows(J - _H4, _H4), c_rx]

        def rx(j):
            return halo4[j + _H4] if j < 0 else z_ref[_rows(j), c_rx]

        for j in range(J):
            acc = c4b_ref[:, cols] + c4w_ref[_rows(0), cols] * rx(j - _H4)
            for k in range(1, CONV4_WIDTH):
                acc = acc + c4w_ref[_rows(k), cols] * rx(j - _H4 + k)
            xc_ref[_rows(j), cols] = acc

        def q(j):
            return z_ref[_rows(j), zcols(o_cc, cols)] * z_ref[_rows(j), zcols(o_cx, cols)]

        tail = [q(J - _H3 + jj) for jj in range(_H3)]
        window = [from_prev_stream(tail[jj], history(pq_ref, jj)) for jj in range(_H3)]
        for jj in range(_H3):
            pq_ref[_rows(jj), cols] = tail[jj]
        for j in range(J):
            qj = q(j) if j < J - _H3 else tail[j - (J - _H3)]
            yc = c3w_ref[_rows(_H3), cols] * qj
            for k in range(_H3):
                yc = yc + c3w_ref[_rows(k), cols] * window[k]
            yc_ref[_rows(j), cols] = yc
            window = window[1:] + [qj]

    def gates(xc_ref, h, cols):
        xh = xc_ref[:, cols]
        xb = xh.astype(jnp.bfloat16)
        slot = lax.rem(g, 2)
        r_cols = slice(2 * cols.start, 2 * cols.start + hb)
        i_cols = slice(2 * cols.start + hb, 2 * cols.stop)
        pre_ref[slot, :, 2 * cols.start:2 * cols.stop] = jnp.dot(
            xb, _weights(wri_ref[h]), preferred_element_type=jnp.float32)
        half_scale = (-0.5 * RG_C) * _softplus(-lam_ref[:, cols])
        for j in range(J):
            t_r = jnp.tanh(0.5 * (pre_ref[slot, _rows(j), r_cols] + br_ref[:, cols]))
            gi = _sigmoid(pre_ref[slot, _rows(j), i_cols] + bi_ref[:, cols])
            log_a = half_scale * t_r + half_scale
            a = jnp.exp(log_a)
            w = -jnp.tanh(log_a) * (1.0 + a * a)
            mult = jnp.where(w > 0.0, w * lax.rsqrt(w), 0.0)
            a_ref[_rows(j), cols] = a
            u_ref[_rows(j), cols] = mult * (gi * xh[_rows(j), :])

    def scan_merge(z_ref, yc_ref, cols, seq_start):
        hloc = jnp.zeros((SUBLANES, hb), jnp.float32)
        prod = jnp.ones((SUBLANES, hb), jnp.float32)
        for j in range(J):
            a = a_ref[_rows(j), cols]
            hloc = a * hloc + u_ref[_rows(j), cols]
            prod = a * prod
        hl_ref[:, cols] = hloc
        pl_ref[:, cols] = prod

        cur = jnp.where(seq_start, 0.0, hc_ref[0:1, cols])
        for s in range(SUBLANES):
            c_ref[s:s + 1, cols] = cur
            cur = hl_ref[s:s + 1, cols] + pl_ref[s:s + 1, cols] * cur
        hc_ref[0:1, cols] = cur
        hstate = c_ref[:, cols]

        for j2 in range(J // 2):
            merged = []
            for j in (2 * j2, 2 * j2 + 1):
                hstate = a_ref[_rows(j), cols] * hstate + u_ref[_rows(j), cols]
                y_rnn = jax.nn.gelu(z_ref[_rows(j), zcols(o_ry, cols)]) * hstate
                y_conv = z_ref[_rows(j), zcols(o_cb, cols)] * yc_ref[_rows(j), cols]
                merged.append(_sigmoid(z_ref[_rows(j), zcols(o_gr, cols)]) * y_rnn
                              + _sigmoid(z_ref[_rows(j), zcols(o_gc, cols)]) * y_conv)
            mrg_ref[_rows(2 * j2, 2), cols] = jnp.concatenate(merged, axis=0).astype(jnp.bfloat16)

    def output(h, cols):
        w_rows = slice(cols.start // 2, cols.stop // 2)
        return jnp.dot(mrg_ref[:, cols], _weights(wout_ref[w_rows, :]),
                       preferred_element_type=jnp.float32)

    starts = [(groups * g + k) % groups_per_seq == 0 for k in range(groups + 1)]
    slots = ((hn0_ref, z0_ref, xc0_ref, yc0_ref), (hn1_ref, z1_ref, xc1_ref, yc1_ref))
    xp_slots = (xp0_ref, xp1_ref)

    def group_rows(k):
        xs = x_ref[k * SUB:(k + 1) * SUB, :] if k < groups else xn_ref[...]
        if not permute_in:
            return xs
        xp_ref = xp_slots[k % 2]
        xp_ref[...] = _to_stream_order(xs)
        return xp_ref[...]

    @pl.when(g == 0)
    def _():
        p4_ref[...] = jnp.zeros_like(p4_ref)
        pq_ref[...] = jnp.zeros_like(pq_ref)
        hc_ref[...] = jnp.zeros_like(hc_ref)
        normalize(group_rows(0), hn0_ref)
        for chunk in range(len(chunks)):
            project(hn0_ref, z0_ref, chunk)
        for cols in heads:
            convs(z0_ref, xc0_ref, yc0_ref, cols, True)

    for k in range(groups):
        rows = slice(k * SUB, (k + 1) * SUB)
        _, z_cur, xc_cur, yc_cur = slots[k % 2]
        hn_nxt, z_nxt, xc_nxt, yc_nxt = slots[(k + 1) % 2]
        normalize(group_rows(k + 1), hn_nxt)
        acc = None
        for h, cols in enumerate(heads):
            gates(xc_cur, h, cols)
            project(hn_nxt, z_nxt, h)
            scan_merge(z_cur, yc_cur, cols, starts[k])
            if h == 1:
                acc = xp_slots[k % 2][...] if permute_in else x_ref[rows, :]
            if h > 0:
                acc = acc + output(h - 1, heads[h - 1])
        o_ref[rows, :] = acc + output(RG_HEADS - 1, heads[-1])
        for cols in heads:
            convs(z_nxt, xc_nxt, yc_nxt, cols, starts[k + 1])


def _ffn_kernel(x_ref, xn_ref, p_ref, gffn_ref, wgu_ref, wd_ref, gple_ref, wpg_ref,
                wple_ref, gfin_ref, o_ref, act0_ref, act1_ref, *,
                groups, d_ff, final):
    g = pl.program_id(0)

    def project(xs, act_ref):
        hn = _rmsnorm(xs, gffn_ref[...]).astype(jnp.bfloat16)
        gu = jnp.dot(hn, _weights(wgu_ref[...]), preferred_element_type=jnp.float32)
        b = GATE_UP_BLOCK
        for t in range(d_ff // b):
            gate = gu[:, 2 * t * b:(2 * t + 1) * b]
            up = gu[:, (2 * t + 1) * b:(2 * t + 2) * b]
            act_ref[:, t * b:(t + 1) * b] = ((gate * _sigmoid(gate)) * up).astype(jnp.bfloat16)

    def down(act_ref, xs):
        return xs + jnp.dot(act_ref[...], _weights(wd_ref[...]), preferred_element_type=jnp.float32)

    def embed(x2, ps):
        hn2 = _rmsnorm(x2, gple_ref[...]).astype(jnp.bfloat16)
        gate = _sigmoid(jnp.dot(hn2, _weights(wpg_ref[...]), preferred_element_type=jnp.float32))
        pb = _to_stream_order(ps).astype(jnp.bfloat16)
        pe = jnp.dot(pb, _weights(wple_ref[...]), preferred_element_type=jnp.float32)
        x3 = x2 + gate * pe
        if final:
            x3 = _to_time_order(_rmsnorm(x3, gfin_ref[...]))
        return x3

    @pl.when(g == 0)
    def _():
        project(x_ref[0:FFN_SUB, :], act0_ref)

    act_slots = (act0_ref, act1_ref)
    for k in range(groups):
        rows = slice(k * FFN_SUB, (k + 1) * FFN_SUB)
        nxt = slice((k + 1) * FFN_SUB, (k + 2) * FFN_SUB)
        x2 = down(act_slots[k % 2], x_ref[rows, :])
        project(x_ref[nxt, :] if k + 1 < groups else xn_ref[...], act_slots[(k + 1) % 2])
        o_ref[rows, :] = embed(x2, p_ref[rows, :])


def _resident(shape):
    zeros = (0,) * len(shape)
    return pl.BlockSpec(shape, lambda g: zeros, pipeline_mode=pl.Buffered(1))


def _resident_layer(stacked_shape, layer):
    index = (layer,) + (0,) * (len(stacked_shape) - 1)
    return pl.BlockSpec((None,) + tuple(stacked_shape[1:]), lambda g: index,
                        pipeline_mode=pl.Buffered(1))


def _row_specs(n, d, sub, groups):
    last_group = n // sub - 1
    cur = pl.BlockSpec((groups * sub, d), lambda g: (g, 0))
    nxt = pl.BlockSpec((sub, d), lambda g: (jnp.minimum(groups * (g + 1), last_group), 0))
    return cur, nxt


def _mixer_call(x2d, gmix, win, c4w, c4b, wri, br, bi, lam, c3w, wout, *, layer, seq_len,
                permute_in):
    n, d = x2d.shape
    w_in_cols = win.shape[-1]
    f32 = jnp.float32
    groups = MIXER_GROUPS
    cur, nxt = _row_specs(n, d, SUB, groups)
    consts = (gmix, win, c4w, c4b, wri, br, bi, lam, c3w, wout)
    stacked = (win, wri, wout)
    return pl.pallas_call(
        functools.partial(_mixer_kernel, groups=groups, groups_per_seq=seq_len // SUB, d=d,
                          permute_in=permute_in),
        grid=(n // (groups * SUB),),
        in_specs=[cur, nxt] + [
            _resident_layer(c.shape, layer) if any(c is s for s in stacked)
            else _resident(c.shape) for c in consts],
        out_specs=cur,
        out_shape=jax.ShapeDtypeStruct((n, d), f32),
        scratch_shapes=[
            pltpu.VMEM((SUB, d), jnp.bfloat16),
            pltpu.VMEM((SUB, d), jnp.bfloat16),
            pltpu.VMEM((SUB, w_in_cols), f32),
            pltpu.VMEM((SUB, w_in_cols), f32),
            pltpu.VMEM((SUB, d), f32),
            pltpu.VMEM((SUB, d), f32),
            pltpu.VMEM((SUB, d), f32),
            pltpu.VMEM((SUB, d), f32),
            pltpu.VMEM((2, SUB, 2 * d), f32),
            pltpu.VMEM((SUB, d), f32),
            pltpu.VMEM((SUB, d), f32),
            pltpu.VMEM((SUBLANES, d), f32),
            pltpu.VMEM((SUBLANES, d), f32),
            pltpu.VMEM((SUB, d), jnp.bfloat16),
            pltpu.VMEM((_H4 * SUBLANES, d), f32),
            pltpu.VMEM((_H3 * SUBLANES, d), f32),
            pltpu.VMEM((SUBLANES, d), f32),
            pltpu.VMEM((SUBLANES, d), f32),
            pltpu.VMEM((SUB, d), f32),
            pltpu.VMEM((SUB, d), f32),
        ],
        compiler_params=pltpu.CompilerParams(
            dimension_semantics=("arbitrary",), vmem_limit_bytes=VMEM_LIMIT_BYTES),
        name="mixer",
    )(x2d, x2d, *consts)


def _ffn_call(x2d, p3d, gffn, wgu, wd, gple, wpg, wple, gfin, *, layer, final):
    n, d = x2d.shape
    d_ff = wgu.shape[-1] // 2
    groups, sub = FFN_GROUPS, FFN_SUB
    cur, nxt = _row_specs(n, d, sub, groups)
    p_spec = pl.BlockSpec((None, groups * sub, p3d.shape[-1]), lambda g: (layer, g, 0))
    consts = (gffn, wgu, wd, gple, wpg, wple, gfin)
    stacked = (wgu, wd, wpg, wple)
    return pl.pallas_call(
        functools.partial(_ffn_kernel, groups=groups, d_ff=d_ff, final=final),
        grid=(n // (groups * sub),),
        in_specs=[cur, nxt, p_spec] + [
            _resident_layer(c.shape, layer) if any(c is s for s in stacked)
            else _resident(c.shape) for c in consts],
        out_specs=cur,
        out_shape=jax.ShapeDtypeStruct((n, d), jnp.float32),
        scratch_shapes=[pltpu.VMEM((sub, d_ff), jnp.bfloat16),
                        pltpu.VMEM((sub, d_ff), jnp.bfloat16)],
        compiler_params=pltpu.CompilerParams(
            dimension_semantics=("arbitrary",), vmem_limit_bytes=VMEM_LIMIT_BYTES),
        name="ffn_ple",
    )(x2d, x2d, p3d, *consts)


def kernel(x, p, g_mix, w_in, conv4_w, conv4_b, w_rg_r, b_rg_r, w_rg_i, b_rg_i, lru_lambda, conv3_w, w_out, g_ffn, w_gate_up, w_down, g_ple, w_ple_gate, w_ple, g_final):
    bsz, seq_len, d = x.shape
    depth = p.shape[0]
    for groups, sub in ((MIXER_GROUPS, SUB), (FFN_GROUPS, FFN_SUB)):
        assert groups % 2 == 0 and seq_len % (groups * sub) == 0
    assert d % (RG_HEADS * LANES) == 0 and FFN_SUB == SUB
    row = lambda v: v.reshape(1, -1)
    rows8 = lambda v: jnp.repeat(v.reshape(-1, v.shape[-1]), SUBLANES, axis=0)

    xs = x.reshape(bsz * seq_len, d)
    win_p, wout_p = _pack_weights(w_in), _pack_weights(w_out)
    w_ri = jnp.concatenate([w_rg_r, w_rg_i], axis=-1)
    wri_p = _pack_weights(w_ri.reshape((-1,) + w_ri.shape[-2:])).reshape(
        w_ri.shape[:2] + (w_ri.shape[2] // 2, w_ri.shape[3]))
    wgu_p = _pack_weights(w_gate_up, interleave_halves=GATE_UP_BLOCK)
    wd_p = _pack_weights(w_down)
    wpg_p, wple_p = _pack_weights(w_ple_gate), _pack_weights(w_ple)

    p3d = p.reshape(depth, bsz * seq_len, -1)

    for i in range(depth):
        xs = _mixer_call(
            xs, row(g_mix[i]), win_p, rows8(conv4_w[i]), rows8(conv4_b[i]),
            wri_p, rows8(b_rg_r[i].reshape(-1)), rows8(b_rg_i[i].reshape(-1)), rows8(lru_lambda[i]), rows8(conv3_w[i]),
            wout_p, layer=i, seq_len=seq_len, permute_in=(i == 0))
        xs = _ffn_call(
            xs, p3d, row(g_ffn[i]), wgu_p, wd_p,
            row(g_ple[i]), wpg_p, wple_p,
            row(g_final), layer=i, final=(i == depth - 1))
    return xs.reshape(bsz, seq_len, d)
```
